```python
import jax, jax.numpy as jnp
from jax import lax
import numpy as np

D_MODEL = 2048
BATCH = 8
SEQ = 2048
DEPTH = 4
DEC_BATCH = 1
DEC_SEQ = 16384
PAST_LEN = 128

HGRN_EXPAND = 128
E_HGRN = D_MODEL // 2
HGRN_HEADS = E_HGRN // HGRN_EXPAND
HGRN_DK = HGRN_EXPAND
HGRN_DV = E_HGRN // HGRN_HEADS
CHUNK = 64
MLA_HEADS = 8
QK_NOPE = 128
QK_ROPE = 64
V_HEAD = 128
E_MLA = MLA_HEADS * V_HEAD
Q_LORA = D_MODEL // 4
KV_LORA = D_MODEL // 8
Q_BLOCK = 128
ROPE_BASE = 10000.0
EPS = 1e-6
IN_WIDTHS = (E_HGRN, E_HGRN, E_HGRN, E_HGRN, E_HGRN, Q_LORA, KV_LORA, QK_ROPE, E_MLA, D_MODEL, D_MODEL)
IN_COLS = sum(IN_WIDTHS)

kernel_name = 'hybrid_hgrn2_mla_encoder'


def rms_norm(x, gain):
    xf = x.astype(jnp.float32)
    y = xf * lax.rsqrt(jnp.mean(xf * xf, axis=-1, keepdims=True) + EPS)
    return (y * gain.astype(jnp.float32)).astype(x.dtype)


def apply_rope(x, positions):
    half = x.shape[-1] // 2
    inv_freq = ROPE_BASE ** (-jnp.arange(half, dtype=jnp.float32) / half)
    ang = positions[:, None] * inv_freq[None, :]
    cos = jnp.cos(ang)[:, None, :]
    sin = jnp.sin(ang)[:, None, :]
    xf = x.astype(jnp.float32)
    x1, x2 = xf[..., :half], xf[..., half:]
    return jnp.concatenate([x1 * cos - x2 * sin, x2 * cos + x1 * sin], axis=-1).astype(x.dtype)


def layer_lower_bounds(lb_logits):
    p = jax.nn.softmax(lb_logits.astype(jnp.float32), axis=0)
    c = jnp.cumsum(p, axis=0)
    return c - c[:1]


def hgrn2_chunk_scan(q, k, v, log_f):
    B, S, H, dk = q.shape
    dv = v.shape[-1]
    n = S // CHUNK

    def to_chunks(t):
        return t.reshape(B, n, CHUNK, H, t.shape[-1]).transpose(1, 0, 3, 2, 4)

    mask = jnp.tril(jnp.ones((CHUNK, CHUNK), dtype=bool))[:, :, None]

    def step(state, inp):
        qc, kc, vc, gc = inp
        b = jnp.cumsum(gc, axis=2)
        o_inter = jnp.einsum('bhtk,bhkv->bhtv', qc * jnp.exp(b), state)
        diff = b[:, :, :, None, :] - b[:, :, None, :, :]
        decay = jnp.exp(jnp.where(mask, diff, -jnp.inf))
        scores = jnp.einsum('bhtk,bhsk,bhtsk->bhts', qc, kc, decay)
        o_intra = jnp.einsum('bhts,bhsv->bhtv', scores, vc)
        b_last = b[:, :, -1:, :]
        new_state = jnp.exp(b_last[:, :, 0, :])[..., None] * state + jnp.einsum(
            'bhsk,bhsv->bhkv', kc * jnp.exp(b_last - b), vc)
        return new_state, o_inter + o_intra

    state0 = jnp.zeros((B, H, dk, dv), jnp.float32)
    _, o = lax.scan(step, state0, (to_chunks(q), to_chunks(k), to_chunks(v), to_chunks(log_f)))
    return o.transpose(1, 0, 3, 2, 4).reshape(B, S, H, dv)


def hgrn2_branch(q_raw, zf, zb, i_raw, gate, lb, norm_gain):
    B, S, _ = q_raw.shape

    def heads(t):
        return t.astype(jnp.float32).reshape(B, S, HGRN_HEADS, -1)

    q = jax.nn.silu(heads(q_raw))
    v = heads(i_raw)

    def gates(z_raw, lb_d):
        z = heads(z_raw)
        lbh = lb_d.reshape(HGRN_HEADS, HGRN_DK)
        log_f = jnp.logaddexp(jnp.log(lbh), jnp.log1p(-lbh) + jax.nn.log_sigmoid(z))
        k = (1.0 - lbh) * jax.nn.sigmoid(-z)
        return k, log_f

    k_f, logf_f = gates(zf, lb[0])
    k_b, logf_b = gates(zb, lb[1])
    o_fwd = hgrn2_chunk_scan(q, k_f, v, logf_f)
    o_bwd = jnp.flip(hgrn2_chunk_scan(jnp.flip(q, 1), jnp.flip(k_b, 1), jnp.flip(v, 1), jnp.flip(logf_b, 1)), 1)
    o = rms_norm(o_fwd + o_bwd, norm_gain).reshape(B, S, E_HGRN)
    return (o * jax.nn.silu(gate.astype(jnp.float32))).astype(q_raw.dtype)


def mla_branch(c_q, c_kv, k_rope_raw, gate, q_norm_gain, w_uq, kv_norm_gain, w_ukv):
    B, S, _ = c_q.shape
    positions = jnp.arange(S, dtype=jnp.float32)
    q = (rms_norm(c_q, q_norm_gain) @ w_uq).reshape(B, S, MLA_HEADS, QK_NOPE + QK_ROPE)
    q_nope = q[..., :QK_NOPE]
    q_rope = apply_rope(q[..., QK_NOPE:], positions)
    kv = (rms_norm(c_kv, kv_norm_gain) @ w_ukv).reshape(B, S, MLA_HEADS, QK_NOPE + V_HEAD)
    k_nope = kv[..., :QK_NOPE]
    v = kv[..., QK_NOPE:]
    k_rope = apply_rope(k_rope_raw[:, :, None, :], positions)[:, :, 0, :]
    scale = (QK_NOPE + QK_ROPE) ** -0.5
    nq = S // Q_BLOCK

    def blocks(t):
        return t.reshape(B, nq, Q_BLOCK, MLA_HEADS, t.shape[-1]).swapaxes(0, 1)

    def attend(blk):
        qn, qr = blk
        s = jnp.einsum('bqhd,bkhd->bhqk', qn, k_nope, preferred_element_type=jnp.float32)
        s = s + jnp.einsum('bqhr,bkr->bhqk', qr, k_rope, preferred_element_type=jnp.float32)
        p = jax.nn.softmax(s * scale, axis=-1).astype(v.dtype)
        return jnp.einsum('bhqk,bkhv->bqhv', p, v)

    o = lax.map(attend, (blocks(q_nope), blocks(q_rope)))
    o = o.swapaxes(0, 1).reshape(B, S, E_MLA)
    return (o.astype(jnp.float32) * jax.nn.silu(gate.astype(jnp.float32))).astype(c_q.dtype)


def hybrid_layer(x, norm_gain, w_in, lb, hgrn_norm_gain, q_norm_gain, w_uq, kv_norm_gain, w_ukv, w_pa, w_pb, w_out):
    h = rms_norm(x, norm_gain)
    proj = h @ w_in
    offsets = np.cumsum(IN_WIDTHS)[:-1].tolist()
    q_a, zf, zb, i_a, g_a, c_q, c_kv, k_r, g_b, m_a, m_b = jnp.split(proj, offsets, axis=-1)
    y_a = hgrn2_branch(q_a, zf, zb, i_a, g_a, lb, hgrn_norm_gain) @ w_pa
    y_b = mla_branch(c_q, c_kv, k_r, g_b, q_norm_gain, w_uq, kv_norm_gain, w_ukv) @ w_pb
    merged = jax.nn.sigmoid(m_a) * y_a + jax.nn.sigmoid(m_b) * y_b
    return x + merged @ w_out


def trunk(x, norm_gain, w_in, lb_logits, hgrn_norm_gain, q_norm_gain, w_uq, kv_norm_gain, w_ukv, w_pa, w_pb, w_out, final_norm_gain):
    lbs = layer_lower_bounds(lb_logits)
    for l in range(DEPTH):
        x = hybrid_layer(x, norm_gain[l], w_in[l], lbs[l], hgrn_norm_gain[l], q_norm_gain[l], w_uq[l],
                         kv_norm_gain[l], w_ukv[l], w_pa[l], w_pb[l], w_out[l])
    return rms_norm(x, final_norm_gain)


def setup_inputs(seed: int = 0) -> dict:
    key = jax.random.key(seed)
    ks = jax.random.split(key, 14)
    f32 = jnp.float32

    def nrm(k, shape, scale):
        return jax.random.normal(k, shape, f32) * scale

    return {
        'x_prompt': nrm(ks[0], (BATCH, SEQ, D_MODEL), 1.0),
        'x_sample': nrm(ks[1], (DEC_BATCH, DEC_SEQ, D_MODEL), 1.0),
        'norm_gain': 1.0 + nrm(ks[2], (DEPTH, D_MODEL), 0.02),
        'w_in': nrm(ks[3], (DEPTH, D_MODEL, IN_COLS), D_MODEL ** -0.5),
        'lb_logits': nrm(ks[4], (DEPTH, 2, E_HGRN), 0.5),
        'hgrn_norm_gain': 1.0 + nrm(ks[5], (DEPTH, HGRN_DV), 0.02),
        'q_norm_gain': 1.0 + nrm(ks[6], (DEPTH, Q_LORA), 0.02),
        'w_uq': nrm(ks[7], (DEPTH, Q_LORA, MLA_HEADS * (QK_NOPE + QK_ROPE)), Q_LORA ** -0.5),
        'kv_norm_gain': 1.0 + nrm(ks[8], (DEPTH, KV_LORA), 0.02),
        'w_ukv': nrm(ks[9], (DEPTH, KV_LORA, MLA_HEADS * (QK_NOPE + V_HEAD)), KV_LORA ** -0.5),
        'w_pa': nrm(ks[10], (DEPTH, E_HGRN, D_MODEL), E_HGRN ** -0.5),
        'w_pb': nrm(ks[11], (DEPTH, E_MLA, D_MODEL), E_MLA ** -0.5),
        'w_out': nrm(ks[12], (DEPTH, D_MODEL, D_MODEL), D_MODEL ** -0.5),
        'final_norm_gain': 1.0 + nrm(ks[13], (D_MODEL,), 0.02),
    }


def reference(x_prompt, x_sample, norm_gain, w_in, lb_logits, hgrn_norm_gain, q_norm_gain, w_uq,
              kv_norm_gain, w_ukv, w_pa, w_pb, w_out, final_norm_gain):
    y_prompt = trunk(x_prompt, norm_gain, w_in, lb_logits, hgrn_norm_gain, q_norm_gain, w_uq,
                     kv_norm_gain, w_ukv, w_pa, w_pb, w_out, final_norm_gain)
    y_sample = trunk(x_sample, norm_gain, w_in, lb_logits, hgrn_norm_gain, q_norm_gain, w_uq,
                     kv_norm_gain, w_ukv, w_pa, w_pb, w_out, final_norm_gain)
    return (y_prompt, y_sample)
```

```python
import functools

import jax
import jax.numpy as jnp
import numpy as np
from jax import lax
from jax.experimental import pallas as pl
from jax.experimental.pallas import tpu as pltpu

F32 = jnp.float32
BF16 = jnp.bfloat16

D_MODEL = 2048
DEPTH = 4
E_HGRN = 1024
HEADS = 8
HEAD_DIM = 128
QK_ROPE = 64
ROPE_HALF = QK_ROPE // 2
Q_LORA = 512
KV_LORA = 256
E_MLA = 1024
ROPE_BASE = 10000.0
EPS = 1e-6
ATTN_SCALE = (HEAD_DIM + QK_ROPE) ** -0.5

PROJ_COLS = 11264
COL_Q, COL_ZF, COL_ZB, COL_I, COL_GA = 4096, 5120, 6144, 7168, 8192
COL_MISC, COL_GB = 9216, 10240
LANE = 128
QK_PAD = 2 * HEAD_DIM

HGRN_CHUNK = 128
HGRN_SUB = 32
VMEM_LIMIT = 56 * 1024 * 1024


def _cparams(sem):
    return pltpu.CompilerParams(dimension_semantics=sem, vmem_limit_bytes=VMEM_LIMIT)


def _lower_bound_kernel(x_ref, o_ref):
    x = x_ref[...]
    e = jnp.exp(x - jnp.max(x, axis=0, keepdims=True))
    p = e / jnp.sum(e, axis=0, keepdims=True)
    c = p[0:1]
    rows = [c]
    for l in range(1, DEPTH):
        c = c + p[l:l + 1]
        rows.append(c)
    o_ref[...] = jnp.concatenate([r - rows[0] for r in rows], axis=0)


def _lower_bounds(lb_logits):
    x = lb_logits.reshape(DEPTH, 2 * E_HGRN)
    return pl.pallas_call(
        _lower_bound_kernel,
        out_shape=jax.ShapeDtypeStruct((DEPTH, 2 * E_HGRN), F32),
        name="lower_bounds",
    )(x)


def _inproj_kernel(x_ref, g_ref, w_ref, o_ref, h_ref):
    @pl.when(pl.program_id(1) == 0)
    def _():
        x = x_ref[...]
        ms = jnp.mean(x * x, axis=-1, keepdims=True)
        h_ref[...] = (x * lax.rsqrt(ms + EPS) * g_ref[...]).astype(BF16)

    o_ref[...] = jnp.dot(h_ref[...], w_ref[...], preferred_element_type=F32)


def _inproj(x, gain, w, tm=1024, tn=1024):
    T = x.shape[0]
    tm = min(tm, T)
    return pl.pallas_call(
        _inproj_kernel,
        grid=(T // tm, PROJ_COLS // tn),
        in_specs=[
            pl.BlockSpec((tm, D_MODEL), lambda i, j: (i, 0)),
            pl.BlockSpec((1, D_MODEL), lambda i, j: (0, 0)),
            pl.BlockSpec((D_MODEL, tn), lambda i, j: (0, j)),
        ],
        out_specs=pl.BlockSpec((tm, tn), lambda i, j: (i, j)),
        out_shape=jax.ShapeDtypeStruct((T, PROJ_COLS), F32),
        scratch_shapes=[pltpu.VMEM((tm, D_MODEL), BF16)],
        compiler_params=_cparams(("parallel", "arbitrary")),
        name="inproj",
    )(x, gain, w)


def _hgrn_kernel(*refs, reverse, tb):
    if reverse:
        q_ref, z_ref, v_ref, lb_ref, ofwd_ref, gate_ref, gain_ref, o_ref, state_ref = refs
    else:
        q_ref, z_ref, v_ref, lb_ref, o_ref, state_ref = refs
    C, c = HGRN_CHUNK, HGRN_SUB
    nsub = C // c
    nch = tb // C

    @pl.when(pl.program_id(2) == 0)
    def _():
        state_ref[...] = jnp.zeros_like(state_ref)

    lb = lb_ref[...]
    log_lb = jnp.log(lb)
    log_1m_lb = jnp.log1p(-lb)
    one_m_lb = 1.0 - lb

    row = lax.broadcasted_iota(jnp.int32, (C, C), 0)
    col = lax.broadcasted_iota(jnp.int32, (C, C), 1)
    if reverse:
        tri = col >= row
        before = (col // c) > (row // c)
    else:
        tri = col <= row
        before = (col // c) < (row // c)
    tri_f = tri.astype(F32)
    before_f = before.astype(F32)

    def chunk(ci, carry):
        cc = (nch - 1 - ci) if reverse else ci
        off = pl.multiple_of(cc * C, C)
        z = z_ref[pl.ds(off, C), :]
        q_raw = q_ref[pl.ds(off, C), :]
        v = v_ref[pl.ds(off, C), :]

        e = jnp.exp(-jnp.abs(z))
        r = 1.0 / (1.0 + e)
        log_sig = jnp.minimum(z, 0.0) - jnp.log1p(e)
        k = one_m_lb * jnp.where(z >= 0.0, e * r, r)
        t = log_1m_lb + log_sig
        g = jnp.maximum(log_lb, t) + jnp.log1p(jnp.exp(-jnp.abs(log_lb - t)))
        q = q_raw * (1.0 / (1.0 + jnp.exp(-q_raw)))

        b = jnp.dot(tri_f, g, precision=lax.Precision.HIGHEST, preferred_element_type=F32)
        ref = jnp.dot(before_f, g, precision=lax.Precision.HIGHEST, preferred_element_type=F32)
        b_last = b[0:1] if reverse else b[C - 1:C]

        q_sub = (q * jnp.exp(b - ref)).astype(BF16)
        rows = []
        for i in range(nsub):
            ref_i = ref[i * c:i * c + 1]
            k_i = (k * jnp.exp(ref_i - b)).astype(BF16)
            rows.append(lax.dot_general(q_sub[i * c:(i + 1) * c], k_i, (((1,), (1,)), ((), ())),
                                        preferred_element_type=F32))
        scores = jnp.where(tri, jnp.concatenate(rows, axis=0), 0.0).astype(BF16)

        state = state_ref[...]
        o = jnp.dot(scores, v.astype(BF16), preferred_element_type=F32)
        o = o + lax.dot_general((q * jnp.exp(b)).astype(BF16), state.astype(BF16),
                                (((1,), (1,)), ((), ())), preferred_element_type=F32)
        k_end = (k * jnp.exp(b_last - b)).astype(BF16)
        state_ref[...] = state * jnp.exp(b_last) + lax.dot_general(
            v.astype(BF16), k_end, (((0,), (0,)), ((), ())), preferred_element_type=F32)

        if reverse:
            tot = o + ofwd_ref[pl.ds(off, C), :]
            y = tot * lax.rsqrt(jnp.mean(tot * tot, axis=-1, keepdims=True) + EPS) * gain_ref[...]
            gt = gate_ref[pl.ds(off, C), :]
            o_ref[pl.ds(off, C), :] = (y * gt * (1.0 / (1.0 + jnp.exp(-gt)))).astype(o_ref.dtype)
        else:
            o_ref[pl.ds(off, C), :] = o
        return carry

    lax.fori_loop(0, nch, chunk, 0)


def _hgrn_direction(proj, lb, reverse, ofwd=None, gain=None, tb=1024):
    B, S, _ = proj.shape
    tb = min(tb, S)
    nblk = S // tb
    blk = (lambda j: nblk - 1 - j) if reverse else (lambda j: j)

    def col_spec(col0):
        return pl.BlockSpec((None, tb, LANE), lambda b, h, j: (b, blk(j), col0 // LANE + h))

    z_col = COL_ZB if reverse else COL_ZF
    lb_row = 1 if reverse else 0
    in_specs = [col_spec(COL_Q), col_spec(z_col), col_spec(COL_I),
                pl.BlockSpec((None, 1, LANE), lambda b, h, j: (lb_row, 0, h))]
    args = [proj, proj, proj, lb]
    out_spec = pl.BlockSpec((None, tb, LANE), lambda b, h, j: (b, blk(j), h))
    if reverse:
        in_specs += [col_spec(0), col_spec(COL_GA), pl.BlockSpec((1, LANE), lambda b, h, j: (0, 0))]
        args += [ofwd, proj, gain]
        out_dtype = BF16
    else:
        out_dtype = F32
    return pl.pallas_call(
        functools.partial(_hgrn_kernel, reverse=reverse, tb=tb),
        grid=(B, HEADS, nblk),
        in_specs=in_specs,
        out_specs=out_spec,
        out_shape=jax.ShapeDtypeStruct((B, S, E_HGRN), out_dtype),
        scratch_shapes=[pltpu.VMEM((HEAD_DIM, HEAD_DIM), F32)],
        compiler_params=_cparams(("parallel", "parallel", "arbitrary")),
        name="hgrn_bwd" if reverse else "hgrn_fwd",
    )(*args)


def _mla_prep_kernel(p_ref, gq_ref, gkv_ref, wq_ref, wkv_ref, cos_ref, sin_ref,
                     qt_ref, k_ref, vt_ref):
    def rms(x, gain):
        return (x * lax.rsqrt(jnp.mean(x * x, axis=-1, keepdims=True) + EPS) * gain).astype(BF16)

    blk = p_ref[...]
    cos = cos_ref[...]
    sin = sin_ref[...]
    c_q = rms(blk[:, :Q_LORA], gq_ref[...])
    c_kv = rms(blk[:, Q_LORA:Q_LORA + KV_LORA], gkv_ref[...])
    rope_k = blk[:, Q_LORA + KV_LORA:Q_LORA + KV_LORA + LANE]
    rope_k_swapped = blk[:, Q_LORA + KV_LORA + LANE:]
    k_rot = (rope_k * cos + rope_k_swapped * sin).astype(BF16)

    q_all = jnp.dot(c_q, wq_ref[...], preferred_element_type=F32)
    kv = jnp.dot(c_kv, wkv_ref[...], preferred_element_type=F32)
    for h in range(HEADS):
        base = 3 * LANE * h
        q_nope = q_all[:, base:base + LANE]
        q_rot = q_all[:, base + LANE:base + 2 * LANE] * cos + q_all[:, base + 2 * LANE:base + 3 * LANE] * sin
        q_h = jnp.concatenate([q_nope, q_rot], axis=1) * ATTN_SCALE
        qt_ref[h * QK_PAD:(h + 1) * QK_PAD, :] = q_h.T.astype(BF16)
        k_ref[:, h * QK_PAD:h * QK_PAD + LANE] = kv[:, h * LANE:(h + 1) * LANE].astype(BF16)
        k_ref[:, h * QK_PAD + LANE:(h + 1) * QK_PAD] = k_rot
        vt_ref[h, 0] = kv[:, E_MLA + h * LANE:E_MLA + (h + 1) * LANE].T.astype(BF16)


def _mla_prep(proj, gq, gkv, wq, wkv, cos, sin, tm):
    B, S, _ = proj.shape
    ns = S // tm
    full = lambda shape: pl.BlockSpec(shape, lambda b, i: (0,) * len(shape))
    return pl.pallas_call(
        _mla_prep_kernel,
        grid=(B, ns),
        in_specs=[
            pl.BlockSpec((None, tm, 1024), lambda b, i: (b, i, COL_MISC // 1024)),
            full((1, Q_LORA)), full((1, KV_LORA)),
            full((Q_LORA, HEADS * 3 * LANE)), full((KV_LORA, 2 * E_MLA)),
            pl.BlockSpec((tm, LANE), lambda b, i: (i, 0)),
            pl.BlockSpec((tm, LANE), lambda b, i: (i, 0)),
        ],
        out_specs=[
            pl.BlockSpec((None, HEADS * QK_PAD, tm), lambda b, i: (b, 0, i)),
            pl.BlockSpec((None, tm, HEADS * QK_PAD), lambda b, i: (b, i, 0)),
            pl.BlockSpec((None, HEADS, 1, HEAD_DIM, tm), lambda b, i: (b, 0, i, 0, 0)),
        ],
        out_shape=[
            jax.ShapeDtypeStruct((B, HEADS * QK_PAD, S), BF16),
            jax.ShapeDtypeStruct((B, S, HEADS * QK_PAD), BF16),
            jax.ShapeDtypeStruct((B, HEADS, ns, HEAD_DIM, tm), BF16),
        ],
        compiler_params=_cparams(("parallel", "parallel")),
        name="mla_prep",
    )(proj, gq, gkv, wq, wkv, cos, sin)


def _attn_kernel(qt_ref, k_ref, vt_ref, gate_ref, o_ref, m_ref, l_ref, acc_ref, *, tk):
    nkv = k_ref.shape[0] // tk
    qt = qt_ref[...]
    m_ref[...] = jnp.full_like(m_ref, -jnp.inf)
    l_ref[...] = jnp.zeros_like(l_ref)
    acc_ref[...] = jnp.zeros_like(acc_ref)

    def body(j, carry):
        k = k_ref[pl.ds(pl.multiple_of(j * tk, tk), tk), :]
        st = jnp.dot(k, qt, preferred_element_type=F32)
        m_prev = m_ref[...]
        m_new = jnp.maximum(m_prev, jnp.max(st, axis=0, keepdims=True))
        alpha = jnp.exp(m_prev - m_new)
        p = jnp.exp(st - m_new)
        l_ref[...] = alpha * l_ref[...] + jnp.sum(p, axis=0, keepdims=True)
        acc_ref[...] = alpha * acc_ref[...] + jnp.dot(vt_ref[j], p.astype(BF16),
                                                      preferred_element_type=F32)
        m_ref[...] = m_new
        return carry

    lax.fori_loop(0, nkv, body, 0)
    out = (acc_ref[...] / l_ref[...]).T
    gt = gate_ref[...]
    o_ref[...] = (out * gt * (1.0 / (1.0 + jnp.exp(-gt)))).astype(o_ref.dtype)


def _attention(qt, k, vt, proj, tq, tk):
    B, S, _ = k.shape
    return pl.pallas_call(
        functools.partial(_attn_kernel, tk=tk),
        grid=(B, HEADS, S // tq),
        in_specs=[
            pl.BlockSpec((None, QK_PAD, tq), lambda b, h, i: (b, h, i)),
            pl.BlockSpec((None, S, QK_PAD), lambda b, h, i: (b, 0, h)),
            pl.BlockSpec((None, None, S // tk, HEAD_DIM, tk), lambda b, h, i: (b, h, 0, 0, 0)),
            pl.BlockSpec((None, tq, LANE), lambda b, h, i: (b, i, COL_GB // LANE + h)),
        ],
        out_specs=pl.BlockSpec((None, tq, LANE), lambda b, h, i: (b, i, h)),
        out_shape=jax.ShapeDtypeStruct((B, S, E_MLA), BF16),
        scratch_shapes=[pltpu.VMEM((1, tq), F32), pltpu.VMEM((1, tq), F32),
                        pltpu.VMEM((HEAD_DIM, tq), F32)],
        compiler_params=_cparams(("parallel", "parallel", "arbitrary")),
        name="attention",
    )(qt, k, vt, proj)


def _merge_kernel(x_ref, ya_ref, yb_ref, ma_ref, mb_ref, wpa_ref, wpb_ref, wout_ref, o_ref):
    def sigmoid(t):
        return 1.0 / (1.0 + jnp.exp(-t))

    y_a = jnp.dot(ya_ref[...], wpa_ref[...], preferred_element_type=F32)
    y_b = jnp.dot(yb_ref[...], wpb_ref[...], preferred_element_type=F32)
    merged = sigmoid(ma_ref[...]) * y_a + sigmoid(mb_ref[...]) * y_b
    o_ref[...] = x_ref[...] + jnp.dot(merged.astype(BF16), wout_ref[...], preferred_element_type=F32)


def _merge(x, ya, yb, proj, wpa, wpb, wout, tm=256):
    T = x.shape[0]
    tm = min(tm, T)
    row = lambda width, colblk: pl.BlockSpec((tm, width), lambda i: (i, colblk))
    full = lambda shape: pl.BlockSpec(shape, lambda i: (0, 0))
    return pl.pallas_call(
        _merge_kernel,
        grid=(T // tm,),
        in_specs=[row(D_MODEL, 0), row(E_HGRN, 0), row(E_MLA, 0), row(D_MODEL, 0), row(D_MODEL, 1),
                  full((E_HGRN, D_MODEL)), full((E_MLA, D_MODEL)), full((D_MODEL, D_MODEL))],
        out_specs=row(D_MODEL, 0),
        out_shape=jax.ShapeDtypeStruct((T, D_MODEL), F32),
        compiler_params=_cparams(("parallel",)),
        name="merge",
    )(x, ya, yb, proj, proj, wpa, wpb, wout)


def _final_norm_kernel(x_ref, g_ref, o_ref):
    x = x_ref[...]
    o_ref[...] = x * lax.rsqrt(jnp.mean(x * x, axis=-1, keepdims=True) + EPS) * g_ref[...]


def _final_norm(x, gain, tm=1024):
    T = x.shape[0]
    tm = min(tm, T)
    return pl.pallas_call(
        _final_norm_kernel,
        grid=(T // tm,),
        in_specs=[pl.BlockSpec((tm, D_MODEL), lambda i: (i, 0)),
                  pl.BlockSpec((1, D_MODEL), lambda i: (0, 0))],
        out_specs=pl.BlockSpec((tm, D_MODEL), lambda i: (i, 0)),
        out_shape=jax.ShapeDtypeStruct((T, D_MODEL), F32),
        compiler_params=_cparams(("parallel",)),
        name="final_norm",
    )(x, gain)


def _prep_layer_weights(w_in, w_uq, w_ukv, w_pa, w_pb, w_out):
    o = np.cumsum((0, 1024, 1024, 1024, 1024, 1024, Q_LORA, KV_LORA, QK_ROPE, E_MLA, D_MODEL, D_MODEL))
    piece = lambda n: w_in[:, o[n]:o[n + 1]]
    rope = piece(7)
    zeros = jnp.zeros((D_MODEL, LANE - QK_ROPE), w_in.dtype)
    rope_swapped = jnp.concatenate([rope[:, ROPE_HALF:], rope[:, :ROPE_HALF]], axis=1)
    w_cat = jnp.concatenate(
        [piece(9), piece(10), piece(0), piece(1), piece(2), piece(3), piece(4),
         piece(5), piece(6), rope, zeros, rope_swapped, zeros, piece(8)], axis=1).astype(BF16)

    wq = w_uq.reshape(Q_LORA, HEADS, HEAD_DIM + QK_ROPE)
    r1 = wq[:, :, HEAD_DIM:HEAD_DIM + ROPE_HALF]
    r2 = wq[:, :, HEAD_DIM + ROPE_HALF:]
    zq = jnp.zeros((Q_LORA, HEADS, LANE - QK_ROPE), w_uq.dtype)
    wq = jnp.concatenate([wq[:, :, :HEAD_DIM], r1, r2, zq, r2, r1, zq], axis=2)
    wq = wq.reshape(Q_LORA, HEADS * 3 * LANE).astype(BF16)

    wkv = w_ukv.reshape(KV_LORA, HEADS, 2 * HEAD_DIM)
    wkv = jnp.concatenate([wkv[:, :, :HEAD_DIM].reshape(KV_LORA, E_MLA),
                           wkv[:, :, HEAD_DIM:].reshape(KV_LORA, E_MLA)], axis=1).astype(BF16)
    return w_cat, wq, wkv, w_pa.astype(BF16), w_pb.astype(BF16), w_out.astype(BF16)


def _rope_tables(S):
    inv_freq = ROPE_BASE ** (-jnp.arange(ROPE_HALF, dtype=F32) / ROPE_HALF)
    ang = jnp.arange(S, dtype=F32)[:, None] * inv_freq[None, :]
    cos, sin = jnp.cos(ang), jnp.sin(ang)
    zeros = jnp.zeros((S, LANE - QK_ROPE), F32)
    return (jnp.concatenate([cos, cos, zeros], axis=1),
            jnp.concatenate([-sin, sin, zeros], axis=1))


def _trunk(x, layers, lbs, norm_gain, hgrn_norm_gain, q_norm_gain, kv_norm_gain, final_norm_gain):
    B, S, _ = x.shape
    T = B * S
    cos, sin = _rope_tables(S)
    tk = min(512, S)
    tq = min(512, S)
    xf = x.reshape(T, D_MODEL)
    for l in range(DEPTH):
        w_cat, wq, wkv, wpa, wpb, wout = layers[l]
        proj = _inproj(xf, norm_gain[l].reshape(1, D_MODEL), w_cat)
        proj3 = proj.reshape(B, S, PROJ_COLS)
        lb = lbs[l].reshape(2, 1, E_HGRN)
        o_fwd = _hgrn_direction(proj3, lb, reverse=False)
        ya = _hgrn_direction(proj3, lb, reverse=True, ofwd=o_fwd,
                             gain=hgrn_norm_gain[l].reshape(1, HEAD_DIM))
        qt, k, vt = _mla_prep(proj3, q_norm_gain[l].reshape(1, Q_LORA),
                              kv_norm_gain[l].reshape(1, KV_LORA), wq, wkv, cos, sin, tk)
        yb = _attention(qt, k, vt, proj3, tq, tk)
        xf = _merge(xf, ya.reshape(T, E_HGRN), yb.reshape(T, E_MLA), proj, wpa, wpb, wout)
    return _final_norm(xf, final_norm_gain.reshape(1, D_MODEL)).reshape(B, S, D_MODEL)


def kernel(x_prompt, x_sample, norm_gain, w_in, lb_logits, hgrn_norm_gain, q_norm_gain, w_uq,
           kv_norm_gain, w_ukv, w_pa, w_pb, w_out, final_norm_gain):
    lbs = _lower_bounds(lb_logits)
    layers = [_prep_layer_weights(w_in[l], w_uq[l], w_ukv[l], w_pa[l], w_pb[l], w_out[l])
              for l in range(DEPTH)]
    args = (layers, lbs, norm_gain, hgrn_norm_gain, q_norm_gain, kv_norm_gain, final_norm_gain)
    return (_trunk(x_prompt, *args), _trunk(x_sample, *args))
```

```python
import functools

import jax
import jax.numpy as jnp
import numpy as np
from jax import lax
from jax.experimental import pallas as pl
from jax.experimental.pallas import tpu as pltpu

F32 = jnp.float32
BF16 = jnp.bfloat16

D_MODEL = 2048
DEPTH = 4
E_HGRN = 1024
HEADS = 8
HEAD_DIM = 128
QK_ROPE = 64
ROPE_HALF = QK_ROPE // 2
Q_LORA = 512
KV_LORA = 256
E_MLA = 1024
ROPE_BASE = 10000.0
EPS = 1e-6
ATTN_SCALE = (HEAD_DIM + QK_ROPE) ** -0.5

PROJ_COLS = 11264
COL_Q, COL_ZF, COL_ZB, COL_I, COL_GA = 4096, 5120, 6144, 7168, 8192
COL_MISC, COL_GB = 9216, 10240
LANE = 128
QK_PAD = 2 * HEAD_DIM

HGRN_CHUNK = 128
HGRN_SUB = 32
HGRN_UNROLL = 4
ATTN_GROUP = 256
ATTN_UNROLL = 4
LOG2E = 1.4426950408889634
VMEM_LIMIT = 56 * 1024 * 1024


def _cparams(sem):
    return pltpu.CompilerParams(dimension_semantics=sem, vmem_limit_bytes=VMEM_LIMIT)


def _lower_bound_kernel(x_ref, o_ref):
    x = x_ref[...]
    e = jnp.exp(x - jnp.max(x, axis=0, keepdims=True))
    p = e / jnp.sum(e, axis=0, keepdims=True)
    c = p[0:1]
    rows = [c]
    for l in range(1, DEPTH):
        c = c + p[l:l + 1]
        rows.append(c)
    o_ref[...] = jnp.concatenate([r - rows[0] for r in rows], axis=0)


def _lower_bounds(lb_logits):
    x = lb_logits.reshape(DEPTH, 2 * E_HGRN)
    return pl.pallas_call(
        _lower_bound_kernel,
        out_shape=jax.ShapeDtypeStruct((DEPTH, 2 * E_HGRN), F32),
        name="lower_bounds",
    )(x)


def _inproj_kernel(x_ref, g_ref, w_ref, o_ref, h_ref):
    @pl.when(pl.program_id(1) == 0)
    def _():
        x = x_ref[...]
        ms = jnp.mean(x * x, axis=-1, keepdims=True)
        h_ref[...] = (x * lax.rsqrt(ms + EPS) * g_ref[...]).astype(BF16)

    o_ref[...] = jnp.dot(h_ref[...], w_ref[...], preferred_element_type=F32)


def _inproj(x, gain, w, tm=1024, tn=1024):
    T = x.shape[0]
    tm = min(tm, T)
    return pl.pallas_call(
        _inproj_kernel,
        grid=(T // tm, PROJ_COLS // tn),
        in_specs=[
            pl.BlockSpec((tm, D_MODEL), lambda i, j: (i, 0)),
            pl.BlockSpec((1, D_MODEL), lambda i, j: (0, 0)),
            pl.BlockSpec((D_MODEL, tn), lambda i, j: (0, j)),
        ],
        out_specs=pl.BlockSpec((tm, tn), lambda i, j: (i, j)),
        out_shape=jax.ShapeDtypeStruct((T, PROJ_COLS), F32),
        scratch_shapes=[pltpu.VMEM((tm, D_MODEL), BF16)],
        compiler_params=_cparams(("parallel", "arbitrary")),
        name="inproj",
    )(x, gain, w)


def _hgrn_kernel(*refs, reverse, tb):
    if reverse:
        q_ref, z_ref, v_ref, lb_ref, ofwd_ref, gate_ref, gain_ref, o_ref, state_ref = refs
    else:
        q_ref, z_ref, v_ref, lb_ref, o_ref, state_ref = refs
    C, c = HGRN_CHUNK, HGRN_SUB
    nsub = C // c
    nch = tb // C
    U = min(HGRN_UNROLL, nch)

    @pl.when(pl.program_id(2) == 0)
    def _():
        state_ref[...] = jnp.zeros_like(state_ref)

    lb = lb_ref[...]
    log_lb = jnp.log(lb)
    log_1m_lb = jnp.log1p(-lb)
    one_m_lb = 1.0 - lb

    row = lax.broadcasted_iota(jnp.int32, (C, C), 0)
    col = lax.broadcasted_iota(jnp.int32, (C, C), 1)
    tri = (col >= row) if reverse else (col <= row)
    tri_b = tri.astype(BF16)

    def gates(off):
        z = z_ref[pl.ds(off, C), :]
        q_raw = q_ref[pl.ds(off, C), :]
        v = v_ref[pl.ds(off, C), :].astype(BF16)

        e = jnp.exp(-jnp.abs(z))
        r = 1.0 / (1.0 + e)
        log_sig = jnp.minimum(z, 0.0) - jnp.log(1.0 + e)
        k = one_m_lb * jnp.where(z >= 0.0, e * r, r)
        t = log_1m_lb + log_sig
        g = jnp.maximum(log_lb, t) + jnp.log(1.0 + jnp.exp(-jnp.abs(log_lb - t)))
        q = q_raw * (1.0 / (1.0 + jnp.exp(-q_raw)))

        g = g * LOG2E
        g_hi = g.astype(BF16)
        rem = g - g_hi.astype(F32)
        g_mid = rem.astype(BF16)
        g_lo = (rem - g_mid.astype(F32)).astype(BF16)
        b3 = jnp.dot(tri_b, jnp.concatenate([g_hi, g_mid, g_lo], axis=1), preferred_element_type=F32)
        return q, k, v, b3

    def sub_chunk_scores(q, k, v, b3):
        b = b3[:, :LANE] + b3[:, LANE:2 * LANE] + b3[:, 2 * LANE:]
        b_last = b[0:1] if reverse else b[C - 1:C]
        rows = []
        for i in range(nsub):
            if reverse:
                ref_i = b[(i + 1) * c:(i + 1) * c + 1] if i < nsub - 1 else jnp.zeros_like(b_last)
            else:
                ref_i = b[i * c - 1:i * c] if i > 0 else jnp.zeros_like(b_last)
            blk = slice(i * c, (i + 1) * c)
            q_i = (q[blk] * jnp.exp2(b[blk] - ref_i)).astype(BF16)
            k_i = (k * jnp.exp2(ref_i - b)).astype(BF16)
            rows.append(lax.dot_general(q_i, k_i, (((1,), (1,)), ((), ())), preferred_element_type=F32))
        q_in = (q * jnp.exp2(b)).astype(BF16)
        k_end = (k * jnp.exp2(b_last - b)).astype(BF16)
        return rows, q_in, k_end, jnp.exp2(b_last), v

    def intra(rows, q_in, k_end, decay, v):
        scores = jnp.where(tri, jnp.concatenate(rows, axis=0), 0.0).astype(BF16)
        return jnp.dot(scores, v, preferred_element_type=F32), q_in, k_end, decay, v

    def chunks(ci, carry):
        offs = []
        for u in range(U):
            cu = ci * U + u
            offs.append(pl.multiple_of(((nch - 1 - cu) if reverse else cu) * C, C))
        parts = [gates(off) for off in offs]
        parts = [sub_chunk_scores(*p) for p in parts]
        parts = [intra(*p) for p in parts]
        for off, (o_intra, q_in, k_end, decay, v) in zip(offs, parts):
            state = state_ref[...]
            o = o_intra + lax.dot_general(q_in, state.astype(BF16), (((1,), (1,)), ((), ())),
                                          preferred_element_type=F32)
            state_ref[...] = state * decay + lax.dot_general(
                v, k_end, (((0,), (0,)), ((), ())), preferred_element_type=F32)
            if reverse:
                tot = o + ofwd_ref[pl.ds(off, C), :]
                y = tot * lax.rsqrt(jnp.mean(tot * tot, axis=-1, keepdims=True) + EPS) * gain_ref[...]
                gt = gate_ref[pl.ds(off, C), :]
                o_ref[pl.ds(off, C), :] = (y * gt * (1.0 / (1.0 + jnp.exp(-gt)))).astype(o_ref.dtype)
            else:
                o_ref[pl.ds(off, C), :] = o
        return carry

    lax.fori_loop(0, nch // U, chunks, 0)


def _hgrn_direction(proj, lb, reverse, ofwd=None, gain=None, tb=1024):
    B, S, _ = proj.shape
    tb = min(tb, S)
    nblk = S // tb
    blk = (lambda j: nblk - 1 - j) if reverse else (lambda j: j)

    def col_spec(col0):
        return pl.BlockSpec((None, tb, LANE), lambda b, h, j: (b, blk(j), col0 // LANE + h))

    z_col = COL_ZB if reverse else COL_ZF
    lb_row = 1 if reverse else 0
    in_specs = [col_spec(COL_Q), col_spec(z_col), col_spec(COL_I),
                pl.BlockSpec((None, 1, LANE), lambda b, h, j: (lb_row, 0, h))]
    args = [proj, proj, proj, lb]
    out_spec = pl.BlockSpec((None, tb, LANE), lambda b, h, j: (b, blk(j), h))
    if reverse:
        in_specs += [col_spec(0), col_spec(COL_GA), pl.BlockSpec((1, LANE), lambda b, h, j: (0, 0))]
        args += [ofwd, proj, gain]
        out_dtype = BF16
    else:
        out_dtype = F32
    return pl.pallas_call(
        functools.partial(_hgrn_kernel, reverse=reverse, tb=tb),
        grid=(B, HEADS, nblk),
        in_specs=in_specs,
        out_specs=out_spec,
        out_shape=jax.ShapeDtypeStruct((B, S, E_HGRN), out_dtype),
        scratch_shapes=[pltpu.VMEM((HEAD_DIM, HEAD_DIM), F32)],
        compiler_params=_cparams(("parallel", "parallel", "arbitrary")),
        name="hgrn_bwd" if reverse else "hgrn_fwd",
    )(*args)


def _mla_prep_kernel(p_ref, gq_ref, gkv_ref, wq_ref, wkv_ref, cos_ref, sin_ref,
                     qt_ref, k_ref, vt_ref):
    def rms(x, gain):
        return (x * lax.rsqrt(jnp.mean(x * x, axis=-1, keepdims=True) + EPS) * gain).astype(BF16)

    blk = p_ref[...]
    cos = cos_ref[...]
    sin = sin_ref[...]
    c_q = rms(blk[:, :Q_LORA], gq_ref[...])
    c_kv = rms(blk[:, Q_LORA:Q_LORA + KV_LORA], gkv_ref[...])
    rope_k = blk[:, Q_LORA + KV_LORA:Q_LORA + KV_LORA + LANE]
    rope_k_swapped = blk[:, Q_LORA + KV_LORA + LANE:]
    k_rot = (rope_k * cos + rope_k_swapped * sin).astype(BF16)

    q_all = jnp.dot(c_q, wq_ref[...], preferred_element_type=F32)
    kv = jnp.dot(c_kv, wkv_ref[...], preferred_element_type=F32)
    for h in range(HEADS):
        base = 3 * LANE * h
        q_nope = q_all[:, base:base + LANE]
        q_rot = q_all[:, base + LANE:base + 2 * LANE] * cos + q_all[:, base + 2 * LANE:base + 3 * LANE] * sin
        q_h = jnp.concatenate([q_nope, q_rot], axis=1) * (ATTN_SCALE * LOG2E)
        qt_ref[h * QK_PAD:(h + 1) * QK_PAD, :] = q_h.T.astype(BF16)
        k_ref[:, h * QK_PAD:h * QK_PAD + LANE] = kv[:, h * LANE:(h + 1) * LANE].astype(BF16)
        k_ref[:, h * QK_PAD + LANE:(h + 1) * QK_PAD] = k_rot
        vt_ref[h, 0] = kv[:, E_MLA + h * LANE:E_MLA + (h + 1) * LANE].T.astype(BF16)


def _mla_prep(proj, gq, gkv, wq, wkv, cos, sin, tm):
    B, S, _ = proj.shape
    ns = S // tm
    full = lambda shape: pl.BlockSpec(shape, lambda b, i: (0,) * len(shape))
    return pl.pallas_call(
        _mla_prep_kernel,
        grid=(B, ns),
        in_specs=[
            pl.BlockSpec((None, tm, 1024), lambda b, i: (b, i, COL_MISC // 1024)),
            full((1, Q_LORA)), full((1, KV_LORA)),
            full((Q_LORA, HEADS * 3 * LANE)), full((KV_LORA, 2 * E_MLA)),
            pl.BlockSpec((tm, LANE), lambda b, i: (i, 0)),
            pl.BlockSpec((tm, LANE), lambda b, i: (i, 0)),
        ],
        out_specs=[
            pl.BlockSpec((None, HEADS * QK_PAD, tm), lambda b, i: (b, 0, i)),
            pl.BlockSpec((None, tm, HEADS * QK_PAD), lambda b, i: (b, i, 0)),
            pl.BlockSpec((None, HEADS, 1, HEAD_DIM, tm), lambda b, i: (b, 0, i, 0, 0)),
        ],
        out_shape=[
            jax.ShapeDtypeStruct((B, HEADS * QK_PAD, S), BF16),
            jax.ShapeDtypeStruct((B, S, HEADS * QK_PAD), BF16),
            jax.ShapeDtypeStruct((B, HEADS, ns, HEAD_DIM, tm), BF16),
        ],
        compiler_params=_cparams(("parallel", "parallel")),
        name="mla_prep",
    )(proj, gq, gkv, wq, wkv, cos, sin)


def _attn_kernel(qt_ref, k_ref, vt_ref, gate_ref, o_ref, m_ref, l_ref, acc_ref, s0_ref, s1_ref, *, tk):
    nkv = k_ref.shape[0] // tk
    tq = qt_ref.shape[1]
    groups = [slice(g * ATTN_GROUP, (g + 1) * ATTN_GROUP) for g in range(tq // ATTN_GROUP)]
    m_ref[...] = jnp.full_like(m_ref, -jnp.inf)
    l_ref[...] = jnp.zeros_like(l_ref)
    acc_ref[...] = jnp.zeros_like(acc_ref)

    def scores_into(j, s_ref):
        k = k_ref[pl.ds(pl.multiple_of(j * tk, tk), tk), :]
        for sl in groups:
            s_ref[:, sl] = jnp.dot(k, qt_ref[:, sl], preferred_element_type=F32)

    def softmax_pv(j, s_ref):
        vt = vt_ref[j]
        for sl in groups:
            st = s_ref[:, sl]
            m_prev = m_ref[:, sl]
            m_new = jnp.maximum(m_prev, jnp.max(st, axis=0, keepdims=True))
            alpha = jnp.exp2(m_prev - m_new)
            p = jnp.exp2(st - m_new)
            l_ref[:, sl] = alpha * l_ref[:, sl] + jnp.sum(p, axis=0, keepdims=True)
            acc_ref[:, sl] = alpha * acc_ref[:, sl] + jnp.dot(vt, p.astype(BF16),
                                                              preferred_element_type=F32)
            m_ref[:, sl] = m_new

    bufs = (s0_ref, s1_ref)
    n = min(ATTN_UNROLL, nkv)

    def run_tiles(j0, last):
        for u in range(n):
            if u < n - 1 or not last:
                scores_into(j0 + u + 1, bufs[(u + 1) % 2])
            softmax_pv(j0 + u, bufs[u % 2])

    scores_into(0, s0_ref)

    def body(jj, carry):
        run_tiles(jj * n, False)
        return carry

    lax.fori_loop(0, nkv // n - 1, body, 0)
    run_tiles(nkv - n, True)

    out = (acc_ref[...] / l_ref[...]).T
    gt = gate_ref[...]
    o_ref[...] = (out * gt * (1.0 / (1.0 + jnp.exp(-gt)))).astype(o_ref.dtype)


def _attention(qt, k, vt, proj, tq, tk):
    B, S, _ = k.shape
    return pl.pallas_call(
        functools.partial(_attn_kernel, tk=tk),
        grid=(B, HEADS, S // tq),
        in_specs=[
            pl.BlockSpec((None, QK_PAD, tq), lambda b, h, i: (b, h, i)),
            pl.BlockSpec((None, S, QK_PAD), lambda b, h, i: (b, 0, h)),
            pl.BlockSpec((None, None, S // tk, HEAD_DIM, tk), lambda b, h, i: (b, h, 0, 0, 0)),
            pl.BlockSpec((None, tq, LANE), lambda b, h, i: (b, i, COL_GB // LANE + h)),
        ],
        out_specs=pl.BlockSpec((None, tq, LANE), lambda b, h, i: (b, i, h)),
        out_shape=jax.ShapeDtypeStruct((B, S, E_MLA), BF16),
        scratch_shapes=[pltpu.VMEM((1, tq), F32), pltpu.VMEM((1, tq), F32),
                        pltpu.VMEM((HEAD_DIM, tq), F32),
                        pltpu.VMEM((tk, tq), F32), pltpu.VMEM((tk, tq), F32)],
        compiler_params=_cparams(("parallel", "parallel", "arbitrary")),
        name="attention",
    )(qt, k, vt, proj)


def _merge_kernel(x_ref, ya_ref, yb_ref, ma_ref, mb_ref, wpa_ref, wpb_ref, wout_ref, o_ref):
    def sigmoid(t):
        return 1.0 / (1.0 + jnp.exp(-t))

    y_a = jnp.dot(ya_ref[...], wpa_ref[...], preferred_element_type=F32)
    y_b = jnp.dot(yb_ref[...], wpb_ref[...], preferred_element_type=F32)
    merged = sigmoid(ma_ref[...]) * y_a + sigmoid(mb_ref[...]) * y_b
    o_ref[...] = x_ref[...] + jnp.dot(merged.astype(BF16), wout_ref[...], preferred_element_type=F32)


def _merge(x, ya, yb, proj, wpa, wpb, wout, tm=256):
    T = x.shape[0]
    tm = min(tm, T)
    row = lambda width, colblk: pl.BlockSpec((tm, width), lambda i: (i, colblk))
    full = lambda shape: pl.BlockSpec(shape, lambda i: (0, 0))
    return pl.pallas_call(
        _merge_kernel,
        grid=(T // tm,),
        in_specs=[row(D_MODEL, 0), row(E_HGRN, 0), row(E_MLA, 0), row(D_MODEL, 0), row(D_MODEL, 1),
                  full((E_HGRN, D_MODEL)), full((E_MLA, D_MODEL)), full((D_MODEL, D_MODEL))],
        out_specs=row(D_MODEL, 0),
        out_shape=jax.ShapeDtypeStruct((T, D_MODEL), F32),
        compiler_params=_cparams(("parallel",)),
        name="merge",
    )(x, ya, yb, proj, proj, wpa, wpb, wout)


def _final_norm_kernel(x_ref, g_ref, o_ref):
    x = x_ref[...]
    o_ref[...] = x * lax.rsqrt(jnp.mean(x * x, axis=-1, keepdims=True) + EPS) * g_ref[...]


def _final_norm(x, gain, tm=1024):
    T = x.shape[0]
    tm = min(tm, T)
    return pl.pallas_call(
        _final_norm_kernel,
        grid=(T // tm,),
        in_specs=[pl.BlockSpec((tm, D_MODEL), lambda i: (i, 0)),
                  pl.BlockSpec((1, D_MODEL), lambda i: (0, 0))],
        out_specs=pl.BlockSpec((tm, D_MODEL), lambda i: (i, 0)),
        out_shape=jax.ShapeDtypeStruct((T, D_MODEL), F32),
        compiler_params=_cparams(("parallel",)),
        name="final_norm",
    )(x, gain)


def _prep_layer_weights(w_in, w_uq, w_ukv, w_pa, w_pb, w_out):
    o = np.cumsum((0, 1024, 1024, 1024, 1024, 1024, Q_LORA, KV_LORA, QK_ROPE, E_MLA, D_MODEL, D_MODEL))
    piece = lambda n: w_in[:, o[n]:o[n + 1]]
    rope = piece(7)
    zeros = jnp.zeros((D_MODEL, LANE - QK_ROPE), w_in.dtype)
    rope_swapped = jnp.concatenate([rope[:, ROPE_HALF:], rope[:, :ROPE_HALF]], axis=1)
    w_cat = jnp.concatenate(
        [piece(9), piece(10), piece(0), piece(1), piece(2), piece(3), piece(4),
         piece(5), piece(6), rope, zeros, rope_swapped, zeros, piece(8)], axis=1).astype(BF16)

    wq = w_uq.reshape(Q_LORA, HEADS, HEAD_DIM + QK_ROPE)
    r1 = wq[:, :, HEAD_DIM:HEAD_DIM + ROPE_HALF]
    r2 = wq[:, :, HEAD_DIM + ROPE_HALF:]
    zq = jnp.zeros((Q_LORA, HEADS, LANE - QK_ROPE), w_uq.dtype)
    wq = jnp.concatenate([wq[:, :, :HEAD_DIM], r1, r2, zq, r2, r1, zq], axis=2)
    wq = wq.reshape(Q_LORA, HEADS * 3 * LANE).astype(BF16)

    wkv = w_ukv.reshape(KV_LORA, HEADS, 2 * HEAD_DIM)
    wkv = jnp.concatenate([wkv[:, :, :HEAD_DIM].reshape(KV_LORA, E_MLA),
                           wkv[:, :, HEAD_DIM:].reshape(KV_LORA, E_MLA)], axis=1).astype(BF16)
    return w_cat, wq, wkv, w_pa.astype(BF16), w_pb.astype(BF16), w_out.astype(BF16)


def _rope_tables(S):
    inv_freq = ROPE_BASE ** (-jnp.arange(ROPE_HALF, dtype=F32) / ROPE_HALF)
    ang = jnp.arange(S, dtype=F32)[:, None] * inv_freq[None, :]
    cos, sin = jnp.cos(ang), jnp.sin(ang)
    zeros = jnp.zeros((S, LANE - QK_ROPE), F32)
    return (jnp.concatenate([cos, cos, zeros], axis=1),
            jnp.concatenate([-sin, sin, zeros], axis=1))


def _trunk(x, layers, lbs, norm_gain, hgrn_norm_gain, q_norm_gain, kv_norm_gain, final_norm_gain):
    B, S, _ = x.shape
    T = B * S
    cos, sin = _rope_tables(S)
    tk = min(512, S // 2)
    tq = min(512, S)
    xf = x.reshape(T, D_MODEL)
    for l in range(DEPTH):
        w_cat, wq, wkv, wpa, wpb, wout = layers[l]
        proj = _inproj(xf, norm_gain[l].reshape(1, D_MODEL), w_cat)
        proj3 = proj.reshape(B, S, PROJ_COLS)
        lb = lbs[l].reshape(2, 1, E_HGRN)
        o_fwd = _hgrn_direction(proj3, lb, reverse=False)
        ya = _hgrn_direction(proj3, lb, reverse=True, ofwd=o_fwd,
                             gain=hgrn_norm_gain[l].reshape(1, HEAD_DIM))
        qt, k, vt = _mla_prep(proj3, q_norm_gain[l].reshape(1, Q_LORA),
                              kv_norm_gain[l].reshape(1, KV_LORA), wq, wkv, cos, sin, tk)
        yb = _attention(qt, k, vt, proj3, tq, tk)
        xf = _merge(xf, ya.reshape(T, E_HGRN), yb.reshape(T, E_MLA), proj, wpa, wpb, wout)
    return _final_norm(xf, final_norm_gain.reshape(1, D_MODEL)).reshape(B, S, D_MODEL)


def kernel(x_prompt, x_sample, norm_gain, w_in, lb_logits, hgrn_norm_gain, q_norm_gain, w_uq,
           kv_norm_gain, w_ukv, w_pa, w_pb, w_out, final_norm_gain):
    lbs = _lower_bounds(lb_logits)
    layers = [_prep_layer_weights(w_in[l], w_uq[l], w_ukv[l], w_pa[l], w_pb[l], w_out[l])
              for l in range(DEPTH)]
    args = (layers, lbs, norm_gain, hgrn_norm_gain, q_norm_gain, kv_norm_gain, final_norm_gain)
    return (_trunk(x_prompt, *args), _trunk(x_sample, *args))
```

```python
import functools

import jax
import jax.numpy as jnp
import numpy as np
from jax import lax
from jax.experimental import pallas as pl
from jax.experimental.pallas import tpu as pltpu

F32 = jnp.float32
BF16 = jnp.bfloat16

D_MODEL = 2048
DEPTH = 4
E_HGRN = 1024
HEADS = 8
HEAD_DIM = 128
QK_ROPE = 64
ROPE_HALF = QK_ROPE // 2
Q_LORA = 512
KV_LORA = 256
E_MLA = 1024
ROPE_BASE = 10000.0
EPS = 1e-6
ATTN_SCALE = (HEAD_DIM + QK_ROPE) ** -0.5

PROJ_COLS = 11264
COL_Q, COL_ZF, COL_ZB, COL_I, COL_GA = 4096, 5120, 6144, 7168, 8192
COL_MISC, COL_GB = 9216, 10240
LANE = 128
QK_PAD = 2 * HEAD_DIM

HGRN_CHUNK = 128
HGRN_SUB = 32
HGRN_UNROLL = 4
HGRN_MAX_SUB_DECAY = 120.0
ATTN_GROUP = 256
ATTN_UNROLL = 4
LOG2E = 1.4426950408889634
VMEM_LIMIT = 56 * 1024 * 1024


def _cparams(sem):
    return pltpu.CompilerParams(dimension_semantics=sem, vmem_limit_bytes=VMEM_LIMIT)


def _lower_bound_kernel(x_ref, o_ref):
    x = x_ref[...]
    e = jnp.exp(x - jnp.max(x, axis=0, keepdims=True))
    p = e / jnp.sum(e, axis=0, keepdims=True)
    c = p[0:1]
    rows = [c]
    for l in range(1, DEPTH):
        c = c + p[l:l + 1]
        rows.append(c)
    o_ref[...] = jnp.concatenate([r - rows[0] for r in rows], axis=0)


def _lower_bounds(lb_logits):
    x = lb_logits.reshape(DEPTH, 2 * E_HGRN)
    return pl.pallas_call(
        _lower_bound_kernel,
        out_shape=jax.ShapeDtypeStruct((DEPTH, 2 * E_HGRN), F32),
        name="lower_bounds",
    )(x)


def _inproj_kernel(x_ref, g_ref, w_ref, o_ref, h_ref):
    @pl.when(pl.program_id(1) == 0)
    def _():
        x = x_ref[...]
        ms = jnp.mean(x * x, axis=-1, keepdims=True)
        h_ref[...] = (x * lax.rsqrt(ms + EPS) * g_ref[...]).astype(BF16)

    o_ref[...] = jnp.dot(h_ref[...], w_ref[...], preferred_element_type=F32)


def _inproj(x, gain, w, tm=1024, tn=1024):
    T = x.shape[0]
    tm = min(tm, T)
    return pl.pallas_call(
        _inproj_kernel,
        grid=(T // tm, PROJ_COLS // tn),
        in_specs=[
            pl.BlockSpec((tm, D_MODEL), lambda i, j: (i, 0)),
            pl.BlockSpec((1, D_MODEL), lambda i, j: (0, 0)),
            pl.BlockSpec((D_MODEL, tn), lambda i, j: (0, j)),
        ],
        out_specs=pl.BlockSpec((tm, tn), lambda i, j: (i, j)),
        out_shape=jax.ShapeDtypeStruct((T, PROJ_COLS), F32),
        scratch_shapes=[pltpu.VMEM((tm, D_MODEL), BF16)],
        compiler_params=_cparams(("parallel", "arbitrary")),
        name="inproj",
    )(x, gain, w)


def _hgrn_kernel(*refs, reverse, tb):
    if reverse:
        q_ref, z_ref, v_ref, lb_ref, ofwd_ref, gate_ref, gain_ref, o_ref, state_ref, prev_ref, pair_ref = refs
    else:
        q_ref, z_ref, v_ref, lb_ref, o_ref, state_ref, prev_ref, pair_ref = refs
    C, c = HGRN_CHUNK, HGRN_SUB
    nsub = C // c
    nch = tb // C
    U = min(HGRN_UNROLL, nch)

    @pl.when(pl.program_id(2) == 0)
    def _():
        state_ref[...] = jnp.zeros_like(state_ref)

    lb = lb_ref[...]
    log_lb = jnp.log(lb)
    log_1m_lb = jnp.log1p(-lb)
    one_m_lb = 1.0 - lb

    row = lax.broadcasted_iota(jnp.int32, (C, C), 0)
    col = lax.broadcasted_iota(jnp.int32, (C, C), 1)
    tri = (col >= row) if reverse else (col <= row)
    tri_b = tri.astype(BF16)

    def gates(off):
        z = z_ref[pl.ds(off, C), :]
        q_raw = q_ref[pl.ds(off, C), :]
        v = v_ref[pl.ds(off, C), :].astype(BF16)

        e = jnp.exp(-jnp.abs(z))
        r = 1.0 / (1.0 + e)
        log_sig = jnp.minimum(z, 0.0) - jnp.log(1.0 + e)
        k = one_m_lb * jnp.where(z >= 0.0, e * r, r)
        t = log_1m_lb + log_sig
        g = jnp.maximum(log_lb, t) + jnp.log(1.0 + jnp.exp(-jnp.abs(log_lb - t)))
        q = q_raw * (1.0 / (1.0 + jnp.exp(-q_raw)))

        g = g * LOG2E
        g_hi = g.astype(BF16)
        rem = g - g_hi.astype(F32)
        g_mid = rem.astype(BF16)
        g_lo = (rem - g_mid.astype(F32)).astype(BF16)
        b3 = jnp.dot(tri_b, jnp.concatenate([g_hi, g_mid, g_lo], axis=1), preferred_element_type=F32)
        return q, k, v, b3

    def sub_chunk_scores(q, k, v, b3, exact):
        b = b3[:, :LANE] + b3[:, LANE:2 * LANE] + b3[:, 2 * LANE:]
        b_last = b[0:1] if reverse else b[C - 1:C]
        rows = []
        worst = jnp.zeros_like(b_last)
        for i in range(0 if exact else nsub):
            if reverse:
                ref_i = b[(i + 1) * c:(i + 1) * c + 1] if i < nsub - 1 else jnp.zeros_like(b_last)
                end_i = b[i * c:i * c + 1]
            else:
                ref_i = b[i * c - 1:i * c] if i > 0 else jnp.zeros_like(b_last)
                end_i = b[(i + 1) * c - 1:(i + 1) * c]
            worst = jnp.minimum(worst, end_i - ref_i)
            blk = slice(i * c, (i + 1) * c)
            q_i = (q[blk] * jnp.exp2(b[blk] - ref_i)).astype(BF16)
            k_i = (k * jnp.exp2(ref_i - b)).astype(BF16)
            rows.append(lax.dot_general(q_i, k_i, (((1,), (1,)), ((), ())), preferred_element_type=F32))
        q_in = (q * jnp.exp2(b)).astype(BF16)
        k_end = (k * jnp.exp2(b_last - b)).astype(BF16)
        return (rows, v), (q, k, b, v), worst, (q_in, k_end, jnp.exp2(b_last), v)

    def intra(rows, v):
        scores = jnp.where(tri, jnp.concatenate(rows, axis=0), 0.0).astype(BF16)
        return jnp.dot(scores, v, preferred_element_type=F32)

    def intra_exact(q, k, b, v):
        pair_ref[0] = b
        pair_ref[1] = k
        pair_ref[2] = v.astype(F32)
        t_idx = lax.broadcasted_iota(jnp.int32, (C, 1), 0)

        def source(s, acc):
            b_s = pair_ref[0, pl.ds(s, 1), :]
            w = q * pair_ref[1, pl.ds(s, 1), :] * jnp.exp2(jnp.minimum(b - b_s, 0.0))
            score = jnp.sum(w, axis=1, keepdims=True)
            seen = (t_idx <= s) if reverse else (t_idx >= s)
            return acc + jnp.where(seen, score, 0.0) * pair_ref[2, pl.ds(s, 1), :]

        return lax.fori_loop(0, C, source, jnp.zeros((C, HEAD_DIM), F32))

    def run_chunks(ci, exact):
        offs = []
        for u in range(U):
            cu = ci * U + u
            offs.append(pl.multiple_of(((nch - 1 - cu) if reverse else cu) * C, C))
        parts = [gates(off) for off in offs]
        parts = [sub_chunk_scores(*p, exact) for p in parts]
        worst = parts[0][2]
        for p in parts[1:]:
            worst = jnp.minimum(worst, p[2])
        o_intras = [intra_exact(*p[1]) if exact else intra(*p[0]) for p in parts]
        parts = [(o,) + p[3] for o, p in zip(o_intras, parts)]
        for off, (o_intra, q_in, k_end, decay, v) in zip(offs, parts):
            state = state_ref[...]
            o = o_intra + lax.dot_general(q_in, state.astype(BF16), (((1,), (1,)), ((), ())),
                                          preferred_element_type=F32)
            state_ref[...] = state * decay + lax.dot_general(
                v, k_end, (((0,), (0,)), ((), ())), preferred_element_type=F32)
            if reverse:
                tot = o + ofwd_ref[pl.ds(off, C), :]
                y = tot * lax.rsqrt(jnp.mean(tot * tot, axis=-1, keepdims=True) + EPS) * gain_ref[...]
                gt = gate_ref[pl.ds(off, C), :]
                o_ref[pl.ds(off, C), :] = (y * gt * (1.0 / (1.0 + jnp.exp(-gt)))).astype(o_ref.dtype)
            else:
                o_ref[pl.ds(off, C), :] = o
        return jnp.min(worst)

    def chunks(ci, carry):
        prev_ref[...] = state_ref[...]
        lowest = run_chunks(ci, exact=False)

        @pl.when(jnp.logical_not(lowest >= -HGRN_MAX_SUB_DECAY))
        def _():
            state_ref[...] = prev_ref[...]
            run_chunks(ci, exact=True)

        return carry

    lax.fori_loop(0, nch // U, chunks, 0)


def _hgrn_direction(proj, lb, reverse, ofwd=None, gain=None, tb=1024):
    B, S, _ = proj.shape
    tb = min(tb, S)
    nblk = S // tb
    blk = (lambda j: nblk - 1 - j) if reverse else (lambda j: j)

    def col_spec(col0):
        return pl.BlockSpec((None, tb, LANE), lambda b, h, j: (b, blk(j), col0 // LANE + h))

    z_col = COL_ZB if reverse else COL_ZF
    lb_row = 1 if reverse else 0
    in_specs = [col_spec(COL_Q), col_spec(z_col), col_spec(COL_I),
                pl.BlockSpec((None, 1, LANE), lambda b, h, j: (lb_row, 0, h))]
    args = [proj, proj, proj, lb]
    out_spec = pl.BlockSpec((None, tb, LANE), lambda b, h, j: (b, blk(j), h))
    if reverse:
        in_specs += [col_spec(0), col_spec(COL_GA), pl.BlockSpec((1, LANE), lambda b, h, j: (0, 0))]
        args += [ofwd, proj, gain]
        out_dtype = BF16
    else:
        out_dtype = F32
    return pl.pallas_call(
        functools.partial(_hgrn_kernel, reverse=reverse, tb=tb),
        grid=(B, HEADS, nblk),
        in_specs=in_specs,
        out_specs=out_spec,
        out_shape=jax.ShapeDtypeStruct((B, S, E_HGRN), out_dtype),
        scratch_shapes=[pltpu.VMEM((HEAD_DIM, HEAD_DIM), F32), pltpu.VMEM((HEAD_DIM, HEAD_DIM), F32),
                        pltpu.VMEM((3, HGRN_CHUNK, HEAD_DIM), F32)],
        compiler_params=_cparams(("parallel", "parallel", "arbitrary")),
        name="hgrn_bwd" if reverse else "hgrn_fwd",
    )(*args)


def _mla_prep_kernel(p_ref, gq_ref, gkv_ref, wq_ref, wkv_ref, cos_ref, sin_ref,
                     qt_ref, k_ref, vt_ref):
    def rms(x, gain):
        return (x * lax.rsqrt(jnp.mean(x * x, axis=-1, keepdims=True) + EPS) * gain).astype(BF16)

    blk = p_ref[...]
    cos = cos_ref[...]
    sin = sin_ref[...]
    c_q = rms(blk[:, :Q_LORA], gq_ref[...])
    c_kv = rms(blk[:, Q_LORA:Q_LORA + KV_LORA], gkv_ref[...])
    rope_k = blk[:, Q_LORA + KV_LORA:Q_LORA + KV_LORA + LANE]
    rope_k_swapped = blk[:, Q_LORA + KV_LORA + LANE:]
    k_rot = (rope_k * cos + rope_k_swapped * sin).astype(BF16)

    q_all = jnp.dot(c_q, wq_ref[...], preferred_element_type=F32)
    kv = jnp.dot(c_kv, wkv_ref[...], preferred_element_type=F32)
    for h in range(HEADS):
        base = 3 * LANE * h
        q_nope = q_all[:, base:base + LANE]
        q_rot = q_all[:, base + LANE:base + 2 * LANE] * cos + q_all[:, base + 2 * LANE:base + 3 * LANE] * sin
        q_h = jnp.concatenate([q_nope, q_rot], axis=1) * (ATTN_SCALE * LOG2E)
        qt_ref[h * QK_PAD:(h + 1) * QK_PAD, :] = q_h.T.astype(BF16)
        k_ref[:, h * QK_PAD:h * QK_PAD + LANE] = kv[:, h * LANE:(h + 1) * LANE].astype(BF16)
        k_ref[:, h * QK_PAD + LANE:(h + 1) * QK_PAD] = k_rot
        vt_ref[h, 0] = kv[:, E_MLA + h * LANE:E_MLA + (h + 1) * LANE].T.astype(BF16)


def _mla_prep(proj, gq, gkv, wq, wkv, cos, sin, tm):
    B, S, _ = proj.shape
    ns = S // tm
    full = lambda shape: pl.BlockSpec(shape, lambda b, i: (0,) * len(shape))
    return pl.pallas_call(
        _mla_prep_kernel,
        grid=(B, ns),
        in_specs=[
            pl.BlockSpec((None, tm, 1024), lambda b, i: (b, i, COL_MISC // 1024)),
            full((1, Q_LORA)), full((1, KV_LORA)),
            full((Q_LORA, HEADS * 3 * LANE)), full((KV_LORA, 2 * E_MLA)),
            pl.BlockSpec((tm, LANE), lambda b, i: (i, 0)),
            pl.BlockSpec((tm, LANE), lambda b, i: (i, 0)),
        ],
        out_specs=[
            pl.BlockSpec((None, HEADS * QK_PAD, tm), lambda b, i: (b, 0, i)),
            pl.BlockSpec((None, tm, HEADS * QK_PAD), lambda b, i: (b, i, 0)),
            pl.BlockSpec((None, HEADS, 1, HEAD_DIM, tm), lambda b, i: (b, 0, i, 0, 0)),
        ],
        out_shape=[
            jax.ShapeDtypeStruct((B, HEADS * QK_PAD, S), BF16),
            jax.ShapeDtypeStruct((B, S, HEADS * QK_PAD), BF16),
            jax.ShapeDtypeStruct((B, HEADS, ns, HEAD_DIM, tm), BF16),
        ],
        compiler_params=_cparams(("parallel", "parallel")),
        name="mla_prep",
    )(proj, gq, gkv, wq, wkv, cos, sin)


def _attn_kernel(qt_ref, k_ref, vt_ref, gate_ref, o_ref, m_ref, l_ref, acc_ref, s0_ref, s1_ref, *, tk):
    nkv = k_ref.shape[0] // tk
    tq = qt_ref.shape[1]
    groups = [slice(g * ATTN_GROUP, (g + 1) * ATTN_GROUP) for g in range(tq // ATTN_GROUP)]
    m_ref[...] = jnp.full_like(m_ref, -jnp.inf)
    l_ref[...] = jnp.zeros_like(l_ref)
    acc_ref[...] = jnp.zeros_like(acc_ref)

    def scores_into(j, s_ref):
        k = k_ref[pl.ds(pl.multiple_of(j * tk, tk), tk), :]
        for sl in groups:
            s_ref[:, sl] = jnp.dot(k, qt_ref[:, sl], preferred_element_type=F32)

    def softmax_pv(j, s_ref):
        vt = vt_ref[j]
        for sl in groups:
            st = s_ref[:, sl]
            m_prev = m_ref[:, sl]
            m_new = jnp.maximum(m_prev, jnp.max(st, axis=0, keepdims=True))
            alpha = jnp.exp2(m_prev - m_new)
            p = jnp.exp2(st - m_new)
            l_ref[:, sl] = alpha * l_ref[:, sl] + jnp.sum(p, axis=0, keepdims=True)
            acc_ref[:, sl] = alpha * acc_ref[:, sl] + jnp.dot(vt, p.astype(BF16),
                                                              preferred_element_type=F32)
            m_ref[:, sl] = m_new

    bufs = (s0_ref, s1_ref)
    n = min(ATTN_UNROLL, nkv)

    def run_tiles(j0, last):
        for u in range(n):
            if u < n - 1 or not last:
                scores_into(j0 + u + 1, bufs[(u + 1) % 2])
            softmax_pv(j0 + u, bufs[u % 2])

    scores_into(0, s0_ref)

    def body(jj, carry):
        run_tiles(jj * n, False)
        return carry

    lax.fori_loop(0, nkv // n - 1, body, 0)
    run_tiles(nkv - n, True)

    out = (acc_ref[...] / l_ref[...]).T
    gt = gate_ref[...]
    o_ref[...] = (out * gt * (1.0 / (1.0 + jnp.exp(-gt)))).astype(o_ref.dtype)


def _attention(qt, k, vt, proj, tq, tk):
    B, S, _ = k.shape
    return pl.pallas_call(
        functools.partial(_attn_kernel, tk=tk),
        grid=(B, HEADS, S // tq),
        in_specs=[
            pl.BlockSpec((None, QK_PAD, tq), lambda b, h, i: (b, h, i)),
            pl.BlockSpec((None, S, QK_PAD), lambda b, h, i: (b, 0, h)),
            pl.BlockSpec((None, None, S // tk, HEAD_DIM, tk), lambda b, h, i: (b, h, 0, 0, 0)),
            pl.BlockSpec((None, tq, LANE), lambda b, h, i: (b, i, COL_GB // LANE + h)),
        ],
        out_specs=pl.BlockSpec((None, tq, LANE), lambda b, h, i: (b, i, h)),
        out_shape=jax.ShapeDtypeStruct((B, S, E_MLA), BF16),
        scratch_shapes=[pltpu.VMEM((1, tq), F32), pltpu.VMEM((1, tq), F32),
                        pltpu.VMEM((HEAD_DIM, tq), F32),
                        pltpu.VMEM((tk, tq), F32), pltpu.VMEM((tk, tq), F32)],
        compiler_params=_cparams(("parallel", "parallel", "arbitrary")),
        name="attention",
    )(qt, k, vt, proj)


def _merge_kernel(x_ref, ya_ref, yb_ref, ma_ref, mb_ref, wpa_ref, wpb_ref, wout_ref, o_ref):
    def sigmoid(t):
        return 1.0 / (1.0 + jnp.exp(-t))

    y_a = jnp.dot(ya_ref[...], wpa_ref[...], preferred_element_type=F32)
    y_b = jnp.dot(yb_ref[...], wpb_ref[...], preferred_element_type=F32)
    merged = sigmoid(ma_ref[...]) * y_a + sigmoid(mb_ref[...]) * y_b
    o_ref[...] = x_ref[...] + jnp.dot(merged.astype(BF16), wout_ref[...], preferred_element_type=F32)


def _merge(x, ya, yb, proj, wpa, wpb, wout, tm=256):
    T = x.shape[0]
    tm = min(tm, T)
    row = lambda width, colblk: pl.BlockSpec((tm, width), lambda i: (i, colblk))
    full = lambda shape: pl.BlockSpec(shape, lambda i: (0, 0))
    return pl.pallas_call(
        _merge_kernel,
        grid=(T // tm,),
        in_specs=[row(D_MODEL, 0), row(E_HGRN, 0), row(E_MLA, 0), row(D_MODEL, 0), row(D_MODEL, 1),
                  full((E_HGRN, D_MODEL)), full((E_MLA, D_MODEL)), full((D_MODEL, D_MODEL))],
        out_specs=row(D_MODEL, 0),
        out_shape=jax.ShapeDtypeStruct((T, D_MODEL), F32),
        compiler_params=_cparams(("parallel",)),
        name="merge",
    )(x, ya, yb, proj, proj, wpa, wpb, wout)


def _final_norm_kernel(x_ref, g_ref, o_ref):
    x = x_ref[...]
    o_ref[...] = x * lax.rsqrt(jnp.mean(x * x, axis=-1, keepdims=True) + EPS) * g_ref[...]


def _final_norm(x, gain, tm=1024):
    T = x.shape[0]
    tm = min(tm, T)
    return pl.pallas_call(
        _final_norm_kernel,
        grid=(T // tm,),
        in_specs=[pl.BlockSpec((tm, D_MODEL), lambda i: (i, 0)),
                  pl.BlockSpec((1, D_MODEL), lambda i: (0, 0))],
        out_specs=pl.BlockSpec((tm, D_MODEL), lambda i: (i, 0)),
        out_shape=jax.ShapeDtypeStruct((T, D_MODEL), F32),
        compiler_params=_cparams(("parallel",)),
        name="final_norm",
    )(x, gain)


def _prep_layer_weights(w_in, w_uq, w_ukv, w_pa, w_pb, w_out):
    o = np.cumsum((0, 1024, 1024, 1024, 1024, 1024, Q_LORA, KV_LORA, QK_ROPE, E_MLA, D_MODEL, D_MODEL))
    piece = lambda n: w_in[:, o[n]:o[n + 1]].astype(BF16)
    rope = piece(7)
    zeros = jnp.zeros((D_MODEL, LANE - QK_ROPE), BF16)
    rope_swapped = jnp.concatenate([rope[:, ROPE_HALF:], rope[:, :ROPE_HALF]], axis=1)
    w_cat = jnp.concatenate(
        [piece(9), piece(10), piece(0), piece(1), piece(2), piece(3), piece(4),
         piece(5), piece(6), rope, zeros, rope_swapped, zeros, piece(8)], axis=1)

    wq = w_uq.reshape(Q_LORA, HEADS, HEAD_DIM + QK_ROPE)
    r1 = wq[:, :, HEAD_DIM:HEAD_DIM + ROPE_HALF]
    r2 = wq[:, :, HEAD_DIM + ROPE_HALF:]
    zq = jnp.zeros((Q_LORA, HEADS, LANE - QK_ROPE), w_uq.dtype)
    wq = jnp.concatenate([wq[:, :, :HEAD_DIM], r1, r2, zq, r2, r1, zq], axis=2)
    wq = wq.reshape(Q_LORA, HEADS * 3 * LANE).astype(BF16)

    wkv = w_ukv.reshape(KV_LORA, HEADS, 2 * HEAD_DIM)
    wkv = jnp.concatenate([wkv[:, :, :HEAD_DIM].reshape(KV_LORA, E_MLA),
                           wkv[:, :, HEAD_DIM:].reshape(KV_LORA, E_MLA)], axis=1).astype(BF16)
    return w_cat, wq, wkv, w_pa.astype(BF16), w_pb.astype(BF16), w_out.astype(BF16)


def _rope_tables(S):
    inv_freq = ROPE_BASE ** (-jnp.arange(ROPE_HALF, dtype=F32) / ROPE_HALF)
    ang = jnp.arange(S, dtype=F32)[:, None] * inv_freq[None, :]
    cos, sin = jnp.cos(ang), jnp.sin(ang)
    zeros = jnp.zeros((S, LANE - QK_ROPE), F32)
    return (jnp.concatenate([cos, cos, zeros], axis=1),
            jnp.concatenate([-sin, sin, zeros], axis=1))


def _trunk(x, layers, lbs, norm_gain, hgrn_norm_gain, q_norm_gain, kv_norm_gain, final_norm_gain):
    B, S, _ = x.shape
    T = B * S
    cos, sin = _rope_tables(S)
    tk = min(512, S // 2)
    tq = min(1024, S)
    xf = x.reshape(T, D_MODEL)
    for l in range(DEPTH):
        w_cat, wq, wkv, wpa, wpb, wout = layers[l]
        proj = _inproj(xf, norm_gain[l].reshape(1, D_MODEL), w_cat)
        proj3 = proj.reshape(B, S, PROJ_COLS)
        lb = lbs[l].reshape(2, 1, E_HGRN)
        o_fwd = _hgrn_direction(proj3, lb, reverse=False)
        ya = _hgrn_direction(proj3, lb, reverse=True, ofwd=o_fwd,
                             gain=hgrn_norm_gain[l].reshape(1, HEAD_DIM))
        qt, k, vt = _mla_prep(proj3, q_norm_gain[l].reshape(1, Q_LORA),
                              kv_norm_gain[l].reshape(1, KV_LORA), wq, wkv, cos, sin, tk)
        yb = _attention(qt, k, vt, proj3, tq, tk)
        xf = _merge(xf, ya.reshape(T, E_HGRN), yb.reshape(T, E_MLA), proj, wpa, wpb, wout)
    return _final_norm(xf, final_norm_gain.reshape(1, D_MODEL)).reshape(B, S, D_MODEL)


def kernel(x_prompt, x_sample, norm_gain, w_in, lb_logits, hgrn_norm_gain, q_norm_gain, w_uq,
           kv_norm_gain, w_ukv, w_pa, w_pb, w_out, final_norm_gain):
    lbs = _lower_bounds(lb_logits)
    layers = [_prep_layer_weights(w_in[l], w_uq[l], w_ukv[l], w_pa[l], w_pb[l], w_out[l])
              for l in range(DEPTH)]
    args = (layers, lbs, norm_gain, hgrn_norm_gain, q_norm_gain, kv_norm_gain, final_norm_gain)
    return (_trunk(x_prompt, *args), _trunk(x_sample, *args))
```

```python
import functools

import jax
import jax.numpy as jnp
import numpy as np
from jax import lax
from jax.experimental import pallas as pl
from jax.experimental.pallas import tpu as pltpu

F32 = jnp.float32
BF16 = jnp.bfloat16

D_MODEL = 2048
DEPTH = 4
E_HGRN = 1024
HEADS = 8
HEAD_DIM = 128
QK_ROPE = 64
ROPE_HALF = QK_ROPE // 2
Q_LORA = 512
KV_LORA = 256
E_MLA = 1024
ROPE_BASE = 10000.0
EPS = 1e-6
ATTN_SCALE = (HEAD_DIM + QK_ROPE) ** -0.5

PROJ_COLS = 11264
COL_Q, COL_ZF, COL_ZB, COL_I, COL_GA = 4096, 5120, 6144, 7168, 8192
COL_MISC, COL_GB = 9216, 10240
LANE = 128
QK_PAD = 2 * HEAD_DIM

HGRN_CHUNK = 128
HGRN_SUB = 32
HGRN_UNROLL = 8
HGRN_MAX_SUB_DECAY = 120.0
ATTN_GROUP = 256
ATTN_UNROLL = 4
ATTN_MIN_ROW_SUM = 2.0 ** -80
LOG2E = 1.4426950408889634
VMEM_LIMIT = 56 * 1024 * 1024


def _cparams(sem):
    return pltpu.CompilerParams(dimension_semantics=sem, vmem_limit_bytes=VMEM_LIMIT)


def _lower_bound_kernel(x_ref, o_ref):
    x = x_ref[...]
    e = jnp.exp(x - jnp.max(x, axis=0, keepdims=True))
    p = e / jnp.sum(e, axis=0, keepdims=True)
    c = p[0:1]
    rows = [c]
    for l in range(1, DEPTH):
        c = c + p[l:l + 1]
        rows.append(c)
    o_ref[...] = jnp.concatenate([r - rows[0] for r in rows], axis=0)


def _lower_bounds(lb_logits):
    x = lb_logits.reshape(DEPTH, 2 * E_HGRN)
    return pl.pallas_call(
        _lower_bound_kernel,
        out_shape=jax.ShapeDtypeStruct((DEPTH, 2 * E_HGRN), F32),
        name="lower_bounds",
    )(x)


def _inproj_kernel(x_ref, g_ref, w_ref, o_ref, h_ref):
    @pl.when(pl.program_id(1) == 0)
    def _():
        x = x_ref[...]
        ms = jnp.mean(x * x, axis=-1, keepdims=True)
        h_ref[...] = (x * lax.rsqrt(ms + EPS) * g_ref[...]).astype(BF16)

    o_ref[...] = jnp.dot(h_ref[...], w_ref[...], preferred_element_type=F32)


def _inproj(x, gain, w, tm=1024, tn=1024):
    T = x.shape[0]
    tm = min(tm, T)
    return pl.pallas_call(
        _inproj_kernel,
        grid=(T // tm, PROJ_COLS // tn),
        in_specs=[
            pl.BlockSpec((tm, D_MODEL), lambda i, j: (i, 0)),
            pl.BlockSpec((1, D_MODEL), lambda i, j: (0, 0)),
            pl.BlockSpec((D_MODEL, tn), lambda i, j: (0, j)),
        ],
        out_specs=pl.BlockSpec((tm, tn), lambda i, j: (i, j)),
        out_shape=jax.ShapeDtypeStruct((T, PROJ_COLS), F32),
        scratch_shapes=[pltpu.VMEM((tm, D_MODEL), BF16)],
        compiler_params=_cparams(("parallel", "arbitrary")),
        name="inproj",
    )(x, gain, w)


def _hgrn_kernel(*refs, reverse, tb):
    if reverse:
        q_ref, z_ref, v_ref, lb_ref, ofwd_ref, gate_ref, gain_ref, o_ref, state_ref, prev_ref, pair_ref = refs
    else:
        q_ref, z_ref, v_ref, lb_ref, o_ref, state_ref, prev_ref, pair_ref = refs
    C, c = HGRN_CHUNK, HGRN_SUB
    nsub = C // c
    nch = tb // C
    U = min(HGRN_UNROLL, nch)

    @pl.when(pl.program_id(2) == 0)
    def _():
        state_ref[...] = jnp.zeros_like(state_ref)

    lb = lb_ref[...]
    log_lb = jnp.log(lb)
    log_1m_lb = jnp.log1p(-lb)
    one_m_lb = 1.0 - lb

    row = lax.broadcasted_iota(jnp.int32, (C, C), 0)
    col = lax.broadcasted_iota(jnp.int32, (C, C), 1)
    tri = (col >= row) if reverse else (col <= row)
    tri_b = tri.astype(BF16)

    def gates(off):
        z = z_ref[pl.ds(off, C), :]
        q_raw = q_ref[pl.ds(off, C), :]
        v = v_ref[pl.ds(off, C), :].astype(BF16)

        e = jnp.exp(-jnp.abs(z))
        r = 1.0 / (1.0 + e)
        log_sig = jnp.minimum(z, 0.0) - jnp.log(1.0 + e)
        k = one_m_lb * jnp.where(z >= 0.0, e * r, r)
        t = log_1m_lb + log_sig
        g = jnp.maximum(log_lb, t) + jnp.log(1.0 + jnp.exp(-jnp.abs(log_lb - t)))
        q = q_raw * (1.0 / (1.0 + jnp.exp(-q_raw)))

        g = g * LOG2E
        g_hi = g.astype(BF16)
        rem = g - g_hi.astype(F32)
        g_mid = rem.astype(BF16)
        g_lo = (rem - g_mid.astype(F32)).astype(BF16)
        b3 = jnp.dot(tri_b, jnp.concatenate([g_hi, g_mid, g_lo], axis=1), preferred_element_type=F32)
        return q, k, v, b3

    def sub_chunk_scores(q, k, v, b3, exact):
        b = b3[:, :LANE] + b3[:, LANE:2 * LANE] + b3[:, 2 * LANE:]
        b_last = b[0:1] if reverse else b[C - 1:C]
        rows = []
        worst = jnp.zeros_like(b_last)
        for i in range(0 if exact else nsub):
            if reverse:
                ref_i = b[(i + 1) * c:(i + 1) * c + 1] if i < nsub - 1 else jnp.zeros_like(b_last)
                end_i = b[i * c:i * c + 1]
            else:
                ref_i = b[i * c - 1:i * c] if i > 0 else jnp.zeros_like(b_last)
                end_i = b[(i + 1) * c - 1:(i + 1) * c]
            worst = jnp.minimum(worst, end_i - ref_i)
            blk = slice(i * c, (i + 1) * c)
            q_i = (q[blk] * jnp.exp2(b[blk] - ref_i)).astype(BF16)
            k_i = (k * jnp.exp2(ref_i - b)).astype(BF16)
            rows.append(lax.dot_general(q_i, k_i, (((1,), (1,)), ((), ())), preferred_element_type=F32))
        q_in = (q * jnp.exp2(b)).astype(BF16)
        k_end = (k * jnp.exp2(b_last - b)).astype(BF16)
        return (rows, v), (q, k, b, v), worst, (q_in, k_end, jnp.exp2(b_last), v)

    def intra(rows, v):
        scores = jnp.where(tri, jnp.concatenate(rows, axis=0), 0.0).astype(BF16)
        return jnp.dot(scores, v, preferred_element_type=F32)

    def intra_exact(q, k, b, v):
        pair_ref[0] = b
        pair_ref[1] = k
        pair_ref[2] = v.astype(F32)
        t_idx = lax.broadcasted_iota(jnp.int32, (C, 1), 0)

        def source(s, acc):
            b_s = pair_ref[0, pl.ds(s, 1), :]
            w = q * pair_ref[1, pl.ds(s, 1), :] * jnp.exp2(jnp.minimum(b - b_s, 0.0))
            score = jnp.sum(w, axis=1, keepdims=True)
            seen = (t_idx <= s) if reverse else (t_idx >= s)
            return acc + jnp.where(seen, score, 0.0) * pair_ref[2, pl.ds(s, 1), :]

        return lax.fori_loop(0, C, source, jnp.zeros((C, HEAD_DIM), F32))

    def run_chunks(ci, exact):
        offs = []
        for u in range(U):
            cu = ci * U + u
            offs.append(pl.multiple_of(((nch - 1 - cu) if reverse else cu) * C, C))
        parts = [gates(off) for off in offs]
        parts = [sub_chunk_scores(*p, exact) for p in parts]
        worst = parts[0][2]
        for p in parts[1:]:
            worst = jnp.minimum(worst, p[2])
        o_intras = [intra_exact(*p[1]) if exact else intra(*p[0]) for p in parts]
        adds = [lax.dot_general(v, k_end, (((0,), (0,)), ((), ())), preferred_element_type=F32)
                for _, k_end, _, v in (p[3] for p in parts)]
        state = state_ref[...]
        for off, o_intra, add, (q_in, _, decay, _) in zip(offs, o_intras, adds, (p[3] for p in parts)):
            o = o_intra + lax.dot_general(q_in, state.astype(BF16), (((1,), (1,)), ((), ())),
                                          preferred_element_type=F32)
            state = state * decay + add
            if reverse:
                tot = o + ofwd_ref[pl.ds(off, C), :]
                y = tot * lax.rsqrt(jnp.mean(tot * tot, axis=-1, keepdims=True) + EPS) * gain_ref[...]
                gt = gate_ref[pl.ds(off, C), :]
                o_ref[pl.ds(off, C), :] = (y * gt * (1.0 / (1.0 + jnp.exp(-gt)))).astype(o_ref.dtype)
            else:
                o_ref[pl.ds(off, C), :] = o
        state_ref[...] = state
        return jnp.min(worst)

    def chunks(ci, carry):
        prev_ref[...] = state_ref[...]
        lowest = run_chunks(ci, exact=False)

        @pl.when(jnp.logical_not(lowest >= -HGRN_MAX_SUB_DECAY))
        def _():
            state_ref[...] = prev_ref[...]
            run_chunks(ci, exact=True)

        return carry

    lax.fori_loop(0, nch // U, chunks, 0)


def _hgrn_direction(proj, lb, reverse, ofwd=None, gain=None, tb=1024):
    B, S, _ = proj.shape
    tb = min(tb, S)
    nblk = S // tb
    blk = (lambda j: nblk - 1 - j) if reverse else (lambda j: j)

    def col_spec(col0):
        return pl.BlockSpec((None, tb, LANE), lambda b, h, j: (b, blk(j), col0 // LANE + h))

    z_col = COL_ZB if reverse else COL_ZF
    lb_row = 1 if reverse else 0
    in_specs = [col_spec(COL_Q), col_spec(z_col), col_spec(COL_I),
                pl.BlockSpec((None, 1, LANE), lambda b, h, j: (lb_row, 0, h))]
    args = [proj, proj, proj, lb]
    out_spec = pl.BlockSpec((None, tb, LANE), lambda b, h, j: (b, blk(j), h))
    if reverse:
        in_specs += [col_spec(0), col_spec(COL_GA), pl.BlockSpec((1, LANE), lambda b, h, j: (0, 0))]
        args += [ofwd, proj, gain]
        out_dtype = BF16
    else:
        out_dtype = F32
    return pl.pallas_call(
        functools.partial(_hgrn_kernel, reverse=reverse, tb=tb),
        grid=(B, HEADS, nblk),
        in_specs=in_specs,
        out_specs=out_spec,
        out_shape=jax.ShapeDtypeStruct((B, S, E_HGRN), out_dtype),
        scratch_shapes=[pltpu.VMEM((HEAD_DIM, HEAD_DIM), F32), pltpu.VMEM((HEAD_DIM, HEAD_DIM), F32),
                        pltpu.VMEM((3, HGRN_CHUNK, HEAD_DIM), F32)],
        compiler_params=_cparams(("parallel", "parallel", "arbitrary")),
        name="hgrn_bwd" if reverse else "hgrn_fwd",
    )(*args)


def _mla_prep_kernel(p_ref, gq_ref, gkv_ref, wq_ref, wkv_ref, cos_ref, sin_ref,
                     qt_ref, k_ref, vt_ref):
    def rms(x, gain):
        return (x * lax.rsqrt(jnp.mean(x * x, axis=-1, keepdims=True) + EPS) * gain).astype(BF16)

    blk = p_ref[...]
    cos = cos_ref[...]
    sin = sin_ref[...]
    c_q = rms(blk[:, :Q_LORA], gq_ref[...])
    c_kv = rms(blk[:, Q_LORA:Q_LORA + KV_LORA], gkv_ref[...])
    rope_k = blk[:, Q_LORA + KV_LORA:Q_LORA + KV_LORA + LANE]
    rope_k_swapped = blk[:, Q_LORA + KV_LORA + LANE:]
    k_rot = (rope_k * cos + rope_k_swapped * sin).astype(BF16)

    q_all = jnp.dot(c_q, wq_ref[...], preferred_element_type=F32)
    kv = jnp.dot(c_kv, wkv_ref[...], preferred_element_type=F32)
    for h in range(HEADS):
        base = 3 * LANE * h
        q_nope = q_all[:, base:base + LANE]
        q_rot = q_all[:, base + LANE:base + 2 * LANE] * cos + q_all[:, base + 2 * LANE:base + 3 * LANE] * sin
        q_h = jnp.concatenate([q_nope, q_rot], axis=1) * (ATTN_SCALE * LOG2E)
        qt_ref[h * QK_PAD:(h + 1) * QK_PAD, :] = q_h.T.astype(BF16)
        k_ref[:, h * QK_PAD:h * QK_PAD + LANE] = kv[:, h * LANE:(h + 1) * LANE].astype(BF16)
        k_ref[:, h * QK_PAD + LANE:(h + 1) * QK_PAD] = k_rot
        vt_ref[h, 0] = kv[:, E_MLA + h * LANE:E_MLA + (h + 1) * LANE].T.astype(BF16)


def _mla_prep(proj, gq, gkv, wq, wkv, cos, sin, tm):
    B, S, _ = proj.shape
    ns = S // tm
    full = lambda shape: pl.BlockSpec(shape, lambda b, i: (0,) * len(shape))
    return pl.pallas_call(
        _mla_prep_kernel,
        grid=(B, ns),
        in_specs=[
            pl.BlockSpec((None, tm, 1024), lambda b, i: (b, i, COL_MISC // 1024)),
            full((1, Q_LORA)), full((1, KV_LORA)),
            full((Q_LORA, HEADS * 3 * LANE)), full((KV_LORA, 2 * E_MLA)),
            pl.BlockSpec((tm, LANE), lambda b, i: (i, 0)),
            pl.BlockSpec((tm, LANE), lambda b, i: (i, 0)),
        ],
        out_specs=[
            pl.BlockSpec((None, HEADS * QK_PAD, tm), lambda b, i: (b, 0, i)),
            pl.BlockSpec((None, tm, HEADS * QK_PAD), lambda b, i: (b, i, 0)),
            pl.BlockSpec((None, HEADS, 1, HEAD_DIM, tm), lambda b, i: (b, 0, i, 0, 0)),
        ],
        out_shape=[
            jax.ShapeDtypeStruct((B, HEADS * QK_PAD, S), BF16),
            jax.ShapeDtypeStruct((B, S, HEADS * QK_PAD), BF16),
            jax.ShapeDtypeStruct((B, HEADS, ns, HEAD_DIM, tm), BF16),
        ],
        compiler_params=_cparams(("parallel", "parallel")),
        name="mla_prep",
    )(proj, gq, gkv, wq, wkv, cos, sin)


def _attn_kernel(qt_ref, k_ref, vt_ref, gate_ref, o_ref, kmax_ref, m_ref, l_ref, acc_ref,
                 s0_ref, s1_ref, *, tk):
    nkv = k_ref.shape[0] // tk
    tq = qt_ref.shape[1]
    groups = [slice(g * ATTN_GROUP, (g + 1) * ATTN_GROUP) for g in range(tq // ATTN_GROUP)]

    def key_tile(j):
        return k_ref[pl.ds(pl.multiple_of(j * tk, tk), tk), :]

    @pl.when(pl.program_id(2) == 0)
    def _():
        def tile_max(j, mx):
            k = key_tile(j).astype(F32)
            return jnp.maximum(mx, jnp.max(jnp.sum(k * k, axis=1, keepdims=True), axis=0, keepdims=True))

        mx = lax.fori_loop(0, nkv, tile_max, jnp.zeros((1, 1), F32))
        kmax_ref[...] = jnp.broadcast_to(jnp.sqrt(mx), kmax_ref.shape)

    q = qt_ref[...].astype(F32)
    m_ref[...] = jnp.sqrt(jnp.sum(q * q, axis=0, keepdims=True)) * kmax_ref[:, 0:1]
    l_ref[...] = jnp.zeros_like(l_ref)
    acc_ref[...] = jnp.zeros_like(acc_ref)

    def scores_into(j, s_ref):
        k = key_tile(j)
        for sl in groups:
            s_ref[:, sl] = jnp.dot(k, qt_ref[:, sl], preferred_element_type=F32)

    def softmax_pv(j, s_ref):
        vt = vt_ref[j]
        for sl in groups:
            p = jnp.exp2(s_ref[:, sl] - m_ref[:, sl])
            l_ref[:, sl] += jnp.sum(p, axis=0, keepdims=True)
            acc_ref[:, sl] += jnp.dot(vt, p.astype(BF16), preferred_element_type=F32)

    bufs = (s0_ref, s1_ref)
    n = min(ATTN_UNROLL, nkv)

    def run_tiles(j0, last):
        for u in range(n):
            if u < n - 1 or not last:
                scores_into(j0 + u + 1, bufs[(u + 1) % 2])
            softmax_pv(j0 + u, bufs[u % 2])

    scores_into(0, s0_ref)

    def body(jj, carry):
        run_tiles(jj * n, False)
        return carry

    lax.fori_loop(0, nkv // n - 1, body, 0)
    run_tiles(nkv - n, True)

    def finish():
        out = (acc_ref[...] / l_ref[...]).T
        gt = gate_ref[...]
        o_ref[...] = (out * gt * (1.0 / (1.0 + jnp.exp(-gt)))).astype(o_ref.dtype)

    finish()

    @pl.when(jnp.logical_not(jnp.min(l_ref[...]) >= ATTN_MIN_ROW_SUM))
    def _():
        m_ref[...] = jnp.full_like(m_ref, -jnp.inf)
        l_ref[...] = jnp.zeros_like(l_ref)
        acc_ref[...] = jnp.zeros_like(acc_ref)

        def online(j, carry):
            scores_into(j, s0_ref)
            vt = vt_ref[j]
            for sl in groups:
                st = s0_ref[:, sl]
                m_prev = m_ref[:, sl]
                m_new = jnp.maximum(m_prev, jnp.max(st, axis=0, keepdims=True))
                alpha = jnp.exp2(m_prev - m_new)
                p = jnp.exp2(st - m_new)
                l_ref[:, sl] = alpha * l_ref[:, sl] + jnp.sum(p, axis=0, keepdims=True)
                acc_ref[:, sl] = alpha * acc_ref[:, sl] + jnp.dot(vt, p.astype(BF16),
                                                                  preferred_element_type=F32)
                m_ref[:, sl] = m_new
            return carry

        lax.fori_loop(0, nkv, online, 0)
        finish()


def _attention(qt, k, vt, proj, tq, tk):
    B, S, _ = k.shape
    return pl.pallas_call(
        functools.partial(_attn_kernel, tk=tk),
        grid=(B, HEADS, S // tq),
        in_specs=[
            pl.BlockSpec((None, QK_PAD, tq), lambda b, h, i: (b, h, i)),
            pl.BlockSpec((None, S, QK_PAD), lambda b, h, i: (b, 0, h)),
            pl.BlockSpec((None, None, S // tk, HEAD_DIM, tk), lambda b, h, i: (b, h, 0, 0, 0)),
            pl.BlockSpec((None, tq, LANE), lambda b, h, i: (b, i, COL_GB // LANE + h)),
        ],
        out_specs=pl.BlockSpec((None, tq, LANE), lambda b, h, i: (b, i, h)),
        out_shape=jax.ShapeDtypeStruct((B, S, E_MLA), BF16),
        scratch_shapes=[pltpu.VMEM((1, LANE), F32), pltpu.VMEM((1, tq), F32), pltpu.VMEM((1, tq), F32),
                        pltpu.VMEM((HEAD_DIM, tq), F32),
                        pltpu.VMEM((tk, tq), F32), pltpu.VMEM((tk, tq), F32)],
        compiler_params=_cparams(("parallel", "parallel", "arbitrary")),
        name="attention",
    )(qt, k, vt, proj)


def _merge_kernel(x_ref, ya_ref, yb_ref, ma_ref, mb_ref, wpa_ref, wpb_ref, wout_ref, o_ref):
    def sigmoid(t):
        return 1.0 / (1.0 + jnp.exp(-t))

    y_a = jnp.dot(ya_ref[...], wpa_ref[...], preferred_element_type=F32)
    y_b = jnp.dot(yb_ref[...], wpb_ref[...], preferred_element_type=F32)
    merged = sigmoid(ma_ref[...]) * y_a + sigmoid(mb_ref[...]) * y_b
    o_ref[...] = x_ref[...] + jnp.dot(merged.astype(BF16), wout_ref[...], preferred_element_type=F32)


def _merge(x, ya, yb, proj, wpa, wpb, wout, tm=256):
    T = x.shape[0]
    tm = min(tm, T)
    row = lambda width, colblk: pl.BlockSpec((tm, width), lambda i: (i, colblk))
    full = lambda shape: pl.BlockSpec(shape, lambda i: (0, 0))
    return pl.pallas_call(
        _merge_kernel,
        grid=(T // tm,),
        in_specs=[row(D_MODEL, 0), row(E_HGRN, 0), row(E_MLA, 0), row(D_MODEL, 0), row(D_MODEL, 1),
                  full((E_HGRN, D_MODEL)), full((E_MLA, D_MODEL)), full((D_MODEL, D_MODEL))],
        out_specs=row(D_MODEL, 0),
        out_shape=jax.ShapeDtypeStruct((T, D_MODEL), F32),
        compiler_params=_cparams(("parallel",)),
        name="merge",
    )(x, ya, yb, proj, proj, wpa, wpb, wout)


def _final_norm_kernel(x_ref, g_ref, o_ref):
    x = x_ref[...]
    o_ref[...] = x * lax.rsqrt(jnp.mean(x * x, axis=-1, keepdims=True) + EPS) * g_ref[...]


def _final_norm(x, gain, tm=1024):
    T = x.shape[0]
    tm = min(tm, T)
    return pl.pallas_call(
        _final_norm_kernel,
        grid=(T // tm,),
        in_specs=[pl.BlockSpec((tm, D_MODEL), lambda i: (i, 0)),
                  pl.BlockSpec((1, D_MODEL), lambda i: (0, 0))],
        out_specs=pl.BlockSpec((tm, D_MODEL), lambda i: (i, 0)),
        out_shape=jax.ShapeDtypeStruct((T, D_MODEL), F32),
        compiler_params=_cparams(("parallel",)),
        name="final_norm",
    )(x, gain)


def _prep_layer_weights(w_in, w_uq, w_ukv, w_pa, w_pb, w_out):
    o = np.cumsum((0, 1024, 1024, 1024, 1024, 1024, Q_LORA, KV_LORA, QK_ROPE, E_MLA, D_MODEL, D_MODEL))
    piece = lambda n: w_in[:, o[n]:o[n + 1]].astype(BF16)
    rope = piece(7)
    zeros = jnp.zeros((D_MODEL, LANE - QK_ROPE), BF16)
    rope_swapped = jnp.concatenate([rope[:, ROPE_HALF:], rope[:, :ROPE_HALF]], axis=1)
    w_cat = jnp.concatenate(
        [piece(9), piece(10), piece(0), piece(1), piece(2), piece(3), piece(4),
         piece(5), piece(6), rope, zeros, rope_swapped, zeros, piece(8)], axis=1)

    wq = w_uq.reshape(Q_LORA, HEADS, HEAD_DIM + QK_ROPE)
    r1 = wq[:, :, HEAD_DIM:HEAD_DIM + ROPE_HALF]
    r2 = wq[:, :, HEAD_DIM + ROPE_HALF:]
    zq = jnp.zeros((Q_LORA, HEADS, LANE - QK_ROPE), w_uq.dtype)
    wq = jnp.concatenate([wq[:, :, :HEAD_DIM], r1, r2, zq, r2, r1, zq], axis=2)
    wq = wq.reshape(Q_LORA, HEADS * 3 * LANE).astype(BF16)

    wkv = w_ukv.reshape(KV_LORA, HEADS, 2 * HEAD_DIM)
    wkv = jnp.concatenate([wkv[:, :, :HEAD_DIM].reshape(KV_LORA, E_MLA),
                           wkv[:, :, HEAD_DIM:].reshape(KV_LORA, E_MLA)], axis=1).astype(BF16)
    return w_cat, wq, wkv, w_pa.astype(BF16), w_pb.astype(BF16), w_out.astype(BF16)


def _rope_tables(S):
    inv_freq = ROPE_BASE ** (-jnp.arange(ROPE_HALF, dtype=F32) / ROPE_HALF)
    ang = jnp.arange(S, dtype=F32)[:, None] * inv_freq[None, :]
    cos, sin = jnp.cos(ang), jnp.sin(ang)
    zeros = jnp.zeros((S, LANE - QK_ROPE), F32)
    return (jnp.concatenate([cos, cos, zeros], axis=1),
            jnp.concatenate([-sin, sin, zeros], axis=1))


def _trunk(x, layers, lbs, norm_gain, hgrn_norm_gain, q_norm_gain, kv_norm_gain, final_norm_gain):
    B, S, _ = x.shape
    T = B * S
    cos, sin = _rope_tables(S)
    tk = min(512, S // 2)
    tq = min(1024, S)
    xf = x.reshape(T, D_MODEL)
    for l in range(DEPTH):
        w_cat, wq, wkv, wpa, wpb, wout = layers[l]
        proj = _inproj(xf, norm_gain[l].reshape(1, D_MODEL), w_cat)
        proj3 = proj.reshape(B, S, PROJ_COLS)
        lb = lbs[l].reshape(2, 1, E_HGRN)
        o_fwd = _hgrn_direction(proj3, lb, reverse=False)
        ya = _hgrn_direction(proj3, lb, reverse=True, ofwd=o_fwd,
                             gain=hgrn_norm_gain[l].reshape(1, HEAD_DIM))
        qt, k, vt = _mla_prep(proj3, q_norm_gain[l].reshape(1, Q_LORA),
                              kv_norm_gain[l].reshape(1, KV_LORA), wq, wkv, cos, sin, tk)
        yb = _attention(qt, k, vt, proj3, tq, tk)
        xf = _merge(xf, ya.reshape(T, E_HGRN), yb.reshape(T, E_MLA), proj, wpa, wpb, wout)
    return _final_norm(xf, final_norm_gain.reshape(1, D_MODEL)).reshape(B, S, D_MODEL)


def kernel(x_prompt, x_sample, norm_gain, w_in, lb_logits, hgrn_norm_gain, q_norm_gain, w_uq,
           kv_norm_gain, w_ukv, w_pa, w_pb, w_out, final_norm_gain):
    lbs = _lower_bounds(lb_logits)
    layers = [_prep_layer_weights(w_in[l], w_uq[l], w_ukv[l], w_pa[l], w_pb[l], w_out[l])
              for l in range(DEPTH)]
    args = (layers, lbs, norm_gain, hgrn_norm_gain, q_norm_gain, kv_norm_gain, final_norm_gain)
    return (_trunk(x_prompt, *args), _trunk(x_sample, *args))
```

```python
import functools

import jax
import jax.numpy as jnp
import numpy as np
from jax import lax
from jax.experimental import pallas as pl
from jax.experimental.pallas import tpu as pltpu

F32 = jnp.float32
BF16 = jnp.bfloat16

D_MODEL = 2048
DEPTH = 4
E_HGRN = 1024
HEADS = 8
HEAD_DIM = 128
QK_ROPE = 64
ROPE_HALF = QK_ROPE // 2
Q_LORA = 512
KV_LORA = 256
E_MLA = 1024
ROPE_BASE = 10000.0
EPS = 1e-6
ATTN_SCALE = (HEAD_DIM + QK_ROPE) ** -0.5

PROJ_COLS = 11264
COL_Q, COL_ZF, COL_ZB, COL_I, COL_GA = 4096, 5120, 6144, 7168, 8192
COL_MISC, COL_GB = 9216, 10240
LANE = 128
QK_PAD = 2 * HEAD_DIM

HGRN_CHUNK = 128
HGRN_SUB = 32
HGRN_UNROLL = 8
HGRN_MAX_SUB_DECAY = 120.0
ATTN_GROUP = 256
ATTN_UNROLL = 4
ATTN_MIN_ROW_SUM = 2.0 ** -80
LOG2E = 1.4426950408889634
VMEM_LIMIT = 56 * 1024 * 1024


def _cparams(sem):
    return pltpu.CompilerParams(dimension_semantics=sem, vmem_limit_bytes=VMEM_LIMIT)


def _lower_bound_kernel(x_ref, o_ref):
    x = x_ref[...]
    e = jnp.exp(x - jnp.max(x, axis=0, keepdims=True))
    p = e / jnp.sum(e, axis=0, keepdims=True)
    c = p[0:1]
    rows = [c]
    for l in range(1, DEPTH):
        c = c + p[l:l + 1]
        rows.append(c)
    o_ref[...] = jnp.concatenate([r - rows[0] for r in rows], axis=0)


def _lower_bounds(lb_logits):
    x = lb_logits.reshape(DEPTH, 2 * E_HGRN)
    return pl.pallas_call(
        _lower_bound_kernel,
        out_shape=jax.ShapeDtypeStruct((DEPTH, 2 * E_HGRN), F32),
        name="lower_bounds",
    )(x)


def _inproj_kernel(x_ref, g_ref, w_ref, o_ref, h_ref):
    @pl.when(pl.program_id(1) == 0)
    def _():
        x = x_ref[...]
        ms = jnp.mean(x * x, axis=-1, keepdims=True)
        h_ref[...] = (x * lax.rsqrt(ms + EPS) * g_ref[...]).astype(BF16)

    o_ref[...] = jnp.dot(h_ref[...], w_ref[...], preferred_element_type=F32)


def _inproj(x, gain, w, tm=1024, tn=1024):
    T = x.shape[0]
    tm = min(tm, T)
    return pl.pallas_call(
        _inproj_kernel,
        grid=(T // tm, PROJ_COLS // tn),
        in_specs=[
            pl.BlockSpec((tm, D_MODEL), lambda i, j: (i, 0)),
            pl.BlockSpec((1, D_MODEL), lambda i, j: (0, 0)),
            pl.BlockSpec((D_MODEL, tn), lambda i, j: (0, j)),
        ],
        out_specs=pl.BlockSpec((tm, tn), lambda i, j: (i, j)),
        out_shape=jax.ShapeDtypeStruct((T, PROJ_COLS), F32),
        scratch_shapes=[pltpu.VMEM((tm, D_MODEL), BF16)],
        compiler_params=_cparams(("parallel", "arbitrary")),
        name="inproj",
    )(x, gain, w)


def _hgrn_kernel(*refs, reverse, tb):
    if reverse:
        q_ref, z_ref, v_ref, lb_ref, ofwd_ref, gate_ref, gain_ref, o_ref, state_ref, prev_ref, pair_ref = refs
    else:
        q_ref, z_ref, v_ref, lb_ref, o_ref, state_ref, prev_ref, pair_ref = refs
    C, c = HGRN_CHUNK, HGRN_SUB
    nsub = C // c
    nch = tb // C
    U = min(HGRN_UNROLL, nch)

    @pl.when(pl.program_id(2) == 0)
    def _():
        state_ref[...] = jnp.zeros_like(state_ref)

    lb = lb_ref[...]
    log_lb = jnp.log(lb) * LOG2E
    log_1m_lb = jnp.log1p(-lb) * LOG2E
    one_m_lb = 1.0 - lb

    row = lax.broadcasted_iota(jnp.int32, (C, C), 0)
    col = lax.broadcasted_iota(jnp.int32, (C, C), 1)
    tri = (col >= row) if reverse else (col <= row)
    tri_b = tri.astype(BF16)

    def gates(off):
        z = z_ref[pl.ds(off, C), :]
        q_raw = q_ref[pl.ds(off, C), :]
        v = v_ref[pl.ds(off, C), :].astype(BF16)

        e = jnp.exp2(jnp.abs(z) * -LOG2E)
        r = 1.0 / (1.0 + e)
        log_sig = jnp.minimum(z, 0.0) * LOG2E - jnp.log2(1.0 + e)
        k = one_m_lb * jnp.where(z >= 0.0, e * r, r)
        t = log_1m_lb + log_sig
        g = jnp.maximum(log_lb, t) + jnp.log2(1.0 + jnp.exp2(-jnp.abs(log_lb - t)))
        q = q_raw * (1.0 / (1.0 + jnp.exp(-q_raw)))

        g_hi = g.astype(BF16)
        rem = g - g_hi.astype(F32)
        g_mid = rem.astype(BF16)
        g_lo = (rem - g_mid.astype(F32)).astype(BF16)
        b3 = jnp.dot(tri_b, jnp.concatenate([g_hi, g_mid, g_lo], axis=1), preferred_element_type=F32)
        return q, k, v, b3

    def sub_chunk_scores(q, k, v, b3, exact):
        b = b3[:, :LANE] + b3[:, LANE:2 * LANE] + b3[:, 2 * LANE:]
        b_last = b[0:1] if reverse else b[C - 1:C]
        rows = []
        worst = jnp.zeros_like(b_last)
        k_bf = k.astype(BF16)
        for i in range(0 if exact else nsub):
            if reverse:
                ref_i = b[(i + 1) * c:(i + 1) * c + 1] if i < nsub - 1 else jnp.zeros_like(b_last)
                end_i = b[i * c:i * c + 1]
            else:
                ref_i = b[i * c - 1:i * c] if i > 0 else jnp.zeros_like(b_last)
                end_i = b[(i + 1) * c - 1:(i + 1) * c]
            worst = jnp.minimum(worst, end_i - ref_i)
            blk = slice(i * c, (i + 1) * c)
            q_i = (q[blk] * jnp.exp2(b[blk] - ref_i)).astype(BF16)
            if reverse:
                k_i = (k[i * c:] * jnp.exp2(ref_i - b[i * c:])).astype(BF16)
                k_i = jnp.concatenate([k_bf[:i * c], k_i], axis=0) if i > 0 else k_i
            else:
                k_i = (k[:(i + 1) * c] * jnp.exp2(ref_i - b[:(i + 1) * c])).astype(BF16)
                k_i = jnp.concatenate([k_i, k_bf[(i + 1) * c:]], axis=0) if i < nsub - 1 else k_i
            rows.append(lax.dot_general(q_i, k_i, (((1,), (1,)), ((), ())), preferred_element_type=F32))
        q_in = (q * jnp.exp2(b)).astype(BF16)
        k_end = (k * jnp.exp2(b_last - b)).astype(BF16)
        return (rows, v), (q, k, b, v), worst, (q_in, k_end, jnp.exp2(b_last), v)

    def intra(rows, v):
        scores = jnp.where(tri, jnp.concatenate(rows, axis=0), 0.0).astype(BF16)
        return jnp.dot(scores, v, preferred_element_type=F32)

    def intra_exact(q, k, b, v):
        pair_ref[0] = b
        pair_ref[1] = k
        pair_ref[2] = v.astype(F32)
        t_idx = lax.broadcasted_iota(jnp.int32, (C, 1), 0)

        def source(s, acc):
            b_s = pair_ref[0, pl.ds(s, 1), :]
            w = q * pair_ref[1, pl.ds(s, 1), :] * jnp.exp2(jnp.minimum(b - b_s, 0.0))
            score = jnp.sum(w, axis=1, keepdims=True)
            seen = (t_idx <= s) if reverse else (t_idx >= s)
            return acc + jnp.where(seen, score, 0.0) * pair_ref[2, pl.ds(s, 1), :]

        return lax.fori_loop(0, C, source, jnp.zeros((C, HEAD_DIM), F32))

    def run_chunks(ci, exact):
        offs = []
        for u in range(U):
            cu = ci * U + u
            offs.append(pl.multiple_of(((nch - 1 - cu) if reverse else cu) * C, C))
        parts = [gates(off) for off in offs]
        parts = [sub_chunk_scores(*p, exact) for p in parts]
        worst = parts[0][2]
        for p in parts[1:]:
            worst = jnp.minimum(worst, p[2])
        o_intras = [intra_exact(*p[1]) if exact else intra(*p[0]) for p in parts]
        adds = [lax.dot_general(v, k_end, (((0,), (0,)), ((), ())), preferred_element_type=F32)
                for _, k_end, _, v in (p[3] for p in parts)]
        state = state_ref[...]
        for off, o_intra, add, (q_in, _, decay, _) in zip(offs, o_intras, adds, (p[3] for p in parts)):
            o = o_intra + lax.dot_general(q_in, state.astype(BF16), (((1,), (1,)), ((), ())),
                                          preferred_element_type=F32)
            state = state * decay + add
            if reverse:
                tot = o + ofwd_ref[pl.ds(off, C), :]
                y = tot * lax.rsqrt(jnp.mean(tot * tot, axis=-1, keepdims=True) + EPS) * gain_ref[...]
                gt = gate_ref[pl.ds(off, C), :]
                o_ref[pl.ds(off, C), :] = (y * gt * (1.0 / (1.0 + jnp.exp(-gt)))).astype(o_ref.dtype)
            else:
                o_ref[pl.ds(off, C), :] = o
        state_ref[...] = state
        return jnp.min(worst)

    def chunks(ci, carry):
        prev_ref[...] = state_ref[...]
        lowest = run_chunks(ci, exact=False)

        @pl.when(jnp.logical_not(lowest >= -HGRN_MAX_SUB_DECAY))
        def _():
            state_ref[...] = prev_ref[...]
            run_chunks(ci, exact=True)

        return carry

    lax.fori_loop(0, nch // U, chunks, 0)


def _hgrn_direction(proj, lb, reverse, ofwd=None, gain=None, tb=1024):
    B, S, _ = proj.shape
    tb = min(tb, S)
    nblk = S // tb
    blk = (lambda j: nblk - 1 - j) if reverse else (lambda j: j)

    def col_spec(col0):
        return pl.BlockSpec((None, tb, LANE), lambda b, h, j: (b, blk(j), col0 // LANE + h))

    z_col = COL_ZB if reverse else COL_ZF
    lb_row = 1 if reverse else 0
    in_specs = [col_spec(COL_Q), col_spec(z_col), col_spec(COL_I),
                pl.BlockSpec((None, 1, LANE), lambda b, h, j: (lb_row, 0, h))]
    args = [proj, proj, proj, lb]
    out_spec = pl.BlockSpec((None, tb, LANE), lambda b, h, j: (b, blk(j), h))
    if reverse:
        in_specs += [col_spec(0), col_spec(COL_GA), pl.BlockSpec((1, LANE), lambda b, h, j: (0, 0))]
        args += [ofwd, proj, gain]
        out_dtype = BF16
    else:
        out_dtype = F32
    return pl.pallas_call(
        functools.partial(_hgrn_kernel, reverse=reverse, tb=tb),
        grid=(B, HEADS, nblk),
        in_specs=in_specs,
        out_specs=out_spec,
        out_shape=jax.ShapeDtypeStruct((B, S, E_HGRN), out_dtype),
        scratch_shapes=[pltpu.VMEM((HEAD_DIM, HEAD_DIM), F32), pltpu.VMEM((HEAD_DIM, HEAD_DIM), F32),
                        pltpu.VMEM((3, HGRN_CHUNK, HEAD_DIM), F32)],
        compiler_params=_cparams(("parallel", "parallel", "arbitrary")),
        name="hgrn_bwd" if reverse else "hgrn_fwd",
    )(*args)


def _mla_prep_kernel(p_ref, gq_ref, gkv_ref, wq_ref, wkv_ref, cos_ref, sin_ref,
                     qt_ref, k_ref, vt_ref):
    def rms(x, gain):
        return (x * lax.rsqrt(jnp.mean(x * x, axis=-1, keepdims=True) + EPS) * gain).astype(BF16)

    blk = p_ref[...]
    cos = cos_ref[...]
    sin = sin_ref[...]
    c_q = rms(blk[:, :Q_LORA], gq_ref[...])
    c_kv = rms(blk[:, Q_LORA:Q_LORA + KV_LORA], gkv_ref[...])
    rope_k = blk[:, Q_LORA + KV_LORA:Q_LORA + KV_LORA + LANE]
    rope_k_swapped = blk[:, Q_LORA + KV_LORA + LANE:]
    k_rot = (rope_k * cos + rope_k_swapped * sin).astype(BF16)

    q_all = jnp.dot(c_q, wq_ref[...], preferred_element_type=F32)
    kv = jnp.dot(c_kv, wkv_ref[...], preferred_element_type=F32)
    for h in range(HEADS):
        base = 3 * LANE * h
        q_nope = q_all[:, base:base + LANE]
        q_rot = q_all[:, base + LANE:base + 2 * LANE] * cos + q_all[:, base + 2 * LANE:base + 3 * LANE] * sin
        q_h = jnp.concatenate([q_nope, q_rot], axis=1) * (ATTN_SCALE * LOG2E)
        qt_ref[h * QK_PAD:(h + 1) * QK_PAD, :] = q_h.T.astype(BF16)
        k_ref[:, h * QK_PAD:h * QK_PAD + LANE] = kv[:, h * LANE:(h + 1) * LANE].astype(BF16)
        k_ref[:, h * QK_PAD + LANE:(h + 1) * QK_PAD] = k_rot
        vt_ref[h, 0] = kv[:, E_MLA + h * LANE:E_MLA + (h + 1) * LANE].T.astype(BF16)


def _mla_prep(proj, gq, gkv, wq, wkv, cos, sin, tm):
    B, S, _ = proj.shape
    ns = S // tm
    full = lambda shape: pl.BlockSpec(shape, lambda b, i: (0,) * len(shape))
    return pl.pallas_call(
        _mla_prep_kernel,
        grid=(B, ns),
        in_specs=[
            pl.BlockSpec((None, tm, 1024), lambda b, i: (b, i, COL_MISC // 1024)),
            full((1, Q_LORA)), full((1, KV_LORA)),
            full((Q_LORA, HEADS * 3 * LANE)), full((KV_LORA, 2 * E_MLA)),
            pl.BlockSpec((tm, LANE), lambda b, i: (i, 0)),
            pl.BlockSpec((tm, LANE), lambda b, i: (i, 0)),
        ],
        out_specs=[
            pl.BlockSpec((None, HEADS * QK_PAD, tm), lambda b, i: (b, 0, i)),
            pl.BlockSpec((None, tm, HEADS * QK_PAD), lambda b, i: (b, i, 0)),
            pl.BlockSpec((None, HEADS, 1, HEAD_DIM, tm), lambda b, i: (b, 0, i, 0, 0)),
        ],
        out_shape=[
            jax.ShapeDtypeStruct((B, HEADS * QK_PAD, S), BF16),
            jax.ShapeDtypeStruct((B, S, HEADS * QK_PAD), BF16),
            jax.ShapeDtypeStruct((B, HEADS, ns, HEAD_DIM, tm), BF16),
        ],
        compiler_params=_cparams(("parallel", "parallel")),
        name="mla_prep",
    )(proj, gq, gkv, wq, wkv, cos, sin)


def _attn_kernel(qt_ref, k_ref, vt_ref, gate_ref, o_ref, kmax_ref, m_ref, l_ref, acc_ref,
                 s0_ref, s1_ref, *, tk):
    nkv = k_ref.shape[0] // tk
    tq = qt_ref.shape[1]
    groups = [slice(g * ATTN_GROUP, (g + 1) * ATTN_GROUP) for g in range(tq // ATTN_GROUP)]

    def key_tile(j):
        return k_ref[pl.ds(pl.multiple_of(j * tk, tk), tk), :]

    @pl.when(pl.program_id(2) == 0)
    def _():
        def tile_max(j, mx):
            k = key_tile(j).astype(F32)
            return jnp.maximum(mx, jnp.max(jnp.sum(k * k, axis=1, keepdims=True), axis=0, keepdims=True))

        mx = lax.fori_loop(0, nkv, tile_max, jnp.zeros((1, 1), F32))
        kmax_ref[...] = jnp.broadcast_to(jnp.sqrt(mx), kmax_ref.shape)

    q = qt_ref[...].astype(F32)
    m_ref[...] = jnp.sqrt(jnp.sum(q * q, axis=0, keepdims=True)) * kmax_ref[:, 0:1]
    l_ref[...] = jnp.zeros_like(l_ref)
    acc_ref[...] = jnp.zeros_like(acc_ref)

    def scores_into(j, s_ref):
        k = key_tile(j)
        for sl in groups:
            s_ref[:, sl] = jnp.dot(k, qt_ref[:, sl], preferred_element_type=F32)

    bufs = (s0_ref, s1_ref)
    n = min(ATTN_UNROLL, nkv)

    def run_tiles(j0, last):
        for u in range(n):
            cur, nxt = bufs[u % 2], bufs[(u + 1) % 2]
            vt = vt_ref[j0 + u]
            k_next = key_tile(j0 + u + 1) if (u < n - 1 or not last) else None
            for sl in groups:
                if k_next is not None:
                    nxt[:, sl] = jnp.dot(k_next, qt_ref[:, sl], preferred_element_type=F32)
                p = jnp.exp2(cur[:, sl] - m_ref[:, sl])
                l_ref[:, sl] += jnp.sum(p, axis=0, keepdims=True)
                acc_ref[:, sl] += jnp.dot(vt, p.astype(BF16), preferred_element_type=F32)

    scores_into(0, s0_ref)

    def body(jj, carry):
        run_tiles(jj * n, False)
        return carry

    lax.fori_loop(0, nkv // n - 1, body, 0)
    run_tiles(nkv - n, True)

    def finish():
        out = (acc_ref[...] / l_ref[...]).T
        gt = gate_ref[...]
        o_ref[...] = (out * gt * (1.0 / (1.0 + jnp.exp(-gt)))).astype(o_ref.dtype)

    finish()

    @pl.when(jnp.logical_not(jnp.min(l_ref[...]) >= ATTN_MIN_ROW_SUM))
    def _():
        m_ref[...] = jnp.full_like(m_ref, -jnp.inf)
        l_ref[...] = jnp.zeros_like(l_ref)
        acc_ref[...] = jnp.zeros_like(acc_ref)

        def online(j, carry):
            scores_into(j, s0_ref)
            vt = vt_ref[j]
            for sl in groups:
                st = s0_ref[:, sl]
                m_prev = m_ref[:, sl]
                m_new = jnp.maximum(m_prev, jnp.max(st, axis=0, keepdims=True))
                alpha = jnp.exp2(m_prev - m_new)
                p = jnp.exp2(st - m_new)
                l_ref[:, sl] = alpha * l_ref[:, sl] + jnp.sum(p, axis=0, keepdims=True)
                acc_ref[:, sl] = alpha * acc_ref[:, sl] + jnp.dot(vt, p.astype(BF16),
                                                                  preferred_element_type=F32)
                m_ref[:, sl] = m_new
            return carry

        lax.fori_loop(0, nkv, online, 0)
        finish()


def _attention(qt, k, vt, proj, tq, tk):
    B, S, _ = k.shape
    return pl.pallas_call(
        functools.partial(_attn_kernel, tk=tk),
        grid=(B, HEADS, S // tq),
        in_specs=[
            pl.BlockSpec((None, QK_PAD, tq), lambda b, h, i: (b, h, i)),
            pl.BlockSpec((None, S, QK_PAD), lambda b, h, i: (b, 0, h)),
            pl.BlockSpec((None, None, S // tk, HEAD_DIM, tk), lambda b, h, i: (b, h, 0, 0, 0)),
            pl.BlockSpec((None, tq, LANE), lambda b, h, i: (b, i, COL_GB // LANE + h)),
        ],
        out_specs=pl.BlockSpec((None, tq, LANE), lambda b, h, i: (b, i, h)),
        out_shape=jax.ShapeDtypeStruct((B, S, E_MLA), BF16),
        scratch_shapes=[pltpu.VMEM((1, LANE), F32), pltpu.VMEM((1, tq), F32), pltpu.VMEM((1, tq), F32),
                        pltpu.VMEM((HEAD_DIM, tq), F32),
                        pltpu.VMEM((tk, tq), F32), pltpu.VMEM((tk, tq), F32)],
        compiler_params=_cparams(("parallel", "parallel", "arbitrary")),
        name="attention",
    )(qt, k, vt, proj)


def _merge_kernel(x_ref, ya_ref, yb_ref, ma_ref, mb_ref, wpa_ref, wpb_ref, wout_ref, gain_ref, o_ref,
                  *, last_layer):
    def sigmoid(t):
        return 1.0 / (1.0 + jnp.exp(-t))

    y_a = jnp.dot(ya_ref[...], wpa_ref[...], preferred_element_type=F32)
    y_b = jnp.dot(yb_ref[...], wpb_ref[...], preferred_element_type=F32)
    merged = sigmoid(ma_ref[...]) * y_a + sigmoid(mb_ref[...]) * y_b
    x = x_ref[...] + jnp.dot(merged.astype(BF16), wout_ref[...], preferred_element_type=F32)
    if last_layer:
        x = x * lax.rsqrt(jnp.mean(x * x, axis=-1, keepdims=True) + EPS) * gain_ref[...]
    o_ref[...] = x


def _merge(x, ya, yb, proj, wpa, wpb, wout, final_gain, last_layer, tm=256):
    T = x.shape[0]
    tm = min(tm, T)
    row = lambda width, colblk: pl.BlockSpec((tm, width), lambda i: (i, colblk))
    full = lambda shape: pl.BlockSpec(shape, lambda i: (0, 0))
    return pl.pallas_call(
        functools.partial(_merge_kernel, last_layer=last_layer),
        grid=(T // tm,),
        in_specs=[row(D_MODEL, 0), row(E_HGRN, 0), row(E_MLA, 0), row(D_MODEL, 0), row(D_MODEL, 1),
                  full((E_HGRN, D_MODEL)), full((E_MLA, D_MODEL)), full((D_MODEL, D_MODEL)),
                  full((1, D_MODEL))],
        out_specs=row(D_MODEL, 0),
        out_shape=jax.ShapeDtypeStruct((T, D_MODEL), F32),
        compiler_params=_cparams(("parallel",)),
        name="merge",
    )(x, ya, yb, proj, proj, wpa, wpb, wout, final_gain)


def _prep_layer_weights(w_in, w_uq, w_ukv, w_pa, w_pb, w_out):
    o = np.cumsum((0, 1024, 1024, 1024, 1024, 1024, Q_LORA, KV_LORA, QK_ROPE, E_MLA, D_MODEL, D_MODEL))
    piece = lambda n: w_in[:, o[n]:o[n + 1]].astype(BF16)
    rope = piece(7)
    zeros = jnp.zeros((D_MODEL, LANE - QK_ROPE), BF16)
    rope_swapped = jnp.concatenate([rope[:, ROPE_HALF:], rope[:, :ROPE_HALF]], axis=1)
    w_cat = jnp.concatenate(
        [piece(9), piece(10), piece(0), piece(1), piece(2), piece(3), piece(4),
         piece(5), piece(6), rope, zeros, rope_swapped, zeros, piece(8)], axis=1)

    wq = w_uq.reshape(Q_LORA, HEADS, HEAD_DIM + QK_ROPE)
    r1 = wq[:, :, HEAD_DIM:HEAD_DIM + ROPE_HALF]
    r2 = wq[:, :, HEAD_DIM + ROPE_HALF:]
    zq = jnp.zeros((Q_LORA, HEADS, LANE - QK_ROPE), w_uq.dtype)
    wq = jnp.concatenate([wq[:, :, :HEAD_DIM], r1, r2, zq, r2, r1, zq], axis=2)
    wq = wq.reshape(Q_LORA, HEADS * 3 * LANE).astype(BF16)

    wkv = w_ukv.reshape(KV_LORA, HEADS, 2 * HEAD_DIM)
    wkv = jnp.concatenate([wkv[:, :, :HEAD_DIM].reshape(KV_LORA, E_MLA),
                           wkv[:, :, HEAD_DIM:].reshape(KV_LORA, E_MLA)], axis=1).astype(BF16)
    return w_cat, wq, wkv, w_pa.astype(BF16), w_pb.astype(BF16), w_out.astype(BF16)


def _rope_tables(S):
    inv_freq = ROPE_BASE ** (-jnp.arange(ROPE_HALF, dtype=F32) / ROPE_HALF)
    ang = jnp.arange(S, dtype=F32)[:, None] * inv_freq[None, :]
    cos, sin = jnp.cos(ang), jnp.sin(ang)
    zeros = jnp.zeros((S, LANE - QK_ROPE), F32)
    return (jnp.concatenate([cos, cos, zeros], axis=1),
            jnp.concatenate([-sin, sin, zeros], axis=1))


def _trunk(x, layers, lbs, norm_gain, hgrn_norm_gain, q_norm_gain, kv_norm_gain, final_norm_gain):
    B, S, _ = x.shape
    T = B * S
    cos, sin = _rope_tables(S)
    tk = min(512, S // 2)
    tq = min(1024, S)
    xf = x.reshape(T, D_MODEL)
    for l in range(DEPTH):
        w_cat, wq, wkv, wpa, wpb, wout = layers[l]
        proj = _inproj(xf, norm_gain[l].reshape(1, D_MODEL), w_cat)
        proj3 = proj.reshape(B, S, PROJ_COLS)
        lb = lbs[l].reshape(2, 1, E_HGRN)
        o_fwd = _hgrn_direction(proj3, lb, reverse=False)
        ya = _hgrn_direction(proj3, lb, reverse=True, ofwd=o_fwd,
                             gain=hgrn_norm_gain[l].reshape(1, HEAD_DIM))
        qt, k, vt = _mla_prep(proj3, q_norm_gain[l].reshape(1, Q_LORA),
                              kv_norm_gain[l].reshape(1, KV_LORA), wq, wkv, cos, sin, tk)
        yb = _attention(qt, k, vt, proj3, tq, tk)
        xf = _merge(xf, ya.reshape(T, E_HGRN), yb.reshape(T, E_MLA), proj, wpa, wpb, wout,
                    final_norm_gain.reshape(1, D_MODEL), last_layer=(l == DEPTH - 1))
    return xf.reshape(B, S, D_MODEL)


def kernel(x_prompt, x_sample, norm_gain, w_in, lb_logits, hgrn_norm_gain, q_norm_gain, w_uq,
           kv_norm_gain, w_ukv, w_pa, w_pb, w_out, final_norm_gain):
    lbs = _lower_bounds(lb_logits)
    layers = [_prep_layer_weights(w_in[l], w_uq[l], w_ukv[l], w_pa[l], w_pb[l], w_out[l])
              for l in range(DEPTH)]
    args = (layers, lbs, norm_gain, hgrn_norm_gain, q_norm_gain, kv_norm_gain, final_norm_gain)
    return (_trunk(x_prompt, *args), _trunk(x_sample, *args))
```

```python
import functools

import jax
import jax.numpy as jnp
import numpy as np
from jax import lax
from jax.experimental import pallas as pl
from jax.experimental.pallas import tpu as pltpu

F32 = jnp.float32
BF16 = jnp.bfloat16

D_MODEL = 2048
DEPTH = 4
E_HGRN = 1024
HEADS = 8
HEAD_DIM = 128
QK_ROPE = 64
ROPE_HALF = QK_ROPE // 2
Q_LORA = 512
KV_LORA = 256
E_MLA = 1024
ROPE_BASE = 10000.0
EPS = 1e-6
ATTN_SCALE = (HEAD_DIM + QK_ROPE) ** -0.5

PROJ_COLS = 11264
COL_Q, COL_ZF, COL_ZB, COL_I, COL_GA = 4096, 5120, 6144, 7168, 8192
COL_MISC, COL_GB = 9216, 10240
LANE = 128
QK_PAD = 2 * HEAD_DIM

HGRN_CHUNK = 128
HGRN_SUB = 32
HGRN_UNROLL = 16
HGRN_MAX_SUB_DECAY = 120.0
ATTN_GROUP = 256
ATTN_UNROLL = 4
ATTN_MIN_ROW_SUM = 2.0 ** -80
LOG2E = 1.4426950408889634
VMEM_LIMIT = 56 * 1024 * 1024


def _cparams(sem):
    return pltpu.CompilerParams(dimension_semantics=sem, vmem_limit_bytes=VMEM_LIMIT)


def _lower_bound_kernel(x_ref, o_ref):
    x = x_ref[...]
    e = jnp.exp(x - jnp.max(x, axis=0, keepdims=True))
    p = e / jnp.sum(e, axis=0, keepdims=True)
    c = p[0:1]
    rows = [c]
    for l in range(1, DEPTH):
        c = c + p[l:l + 1]
        rows.append(c)
    o_ref[...] = jnp.concatenate([r - rows[0] for r in rows], axis=0)


def _lower_bounds(lb_logits):
    x = lb_logits.reshape(DEPTH, 2 * E_HGRN)
    return pl.pallas_call(
        _lower_bound_kernel,
        out_shape=jax.ShapeDtypeStruct((DEPTH, 2 * E_HGRN), F32),
        name="lower_bounds",
    )(x)


IN_OFFSETS = tuple(int(v) for v in np.cumsum(
    (0, E_HGRN, E_HGRN, E_HGRN, E_HGRN, E_HGRN, Q_LORA, KV_LORA, QK_ROPE, E_MLA, D_MODEL, D_MODEL)))
IN_COLS = IN_OFFSETS[-1]


def _pack_w_in_kernel(w_ref, o_ref):
    o = IN_OFFSETS

    def put(dst, first, last):
        o_ref[:, dst:dst + o[last] - o[first]] = w_ref[:, o[first]:o[last]].astype(BF16)

    put(0, 9, 11)
    put(COL_Q, 0, 5)
    put(COL_MISC, 5, 7)
    rope = w_ref[:, o[7]:o[8]]
    zeros = jnp.zeros((rope.shape[0], LANE - QK_ROPE), F32)
    rope_at = COL_MISC + Q_LORA + KV_LORA
    o_ref[:, rope_at:rope_at + LANE] = jnp.concatenate([rope, zeros], axis=1).astype(BF16)
    o_ref[:, rope_at + LANE:rope_at + 2 * LANE] = jnp.concatenate(
        [rope[:, ROPE_HALF:], rope[:, :ROPE_HALF], zeros], axis=1).astype(BF16)
    put(COL_GB, 8, 9)


def _pack_w_in(w_in, tr=256):
    return pl.pallas_call(
        _pack_w_in_kernel,
        grid=(DEPTH, D_MODEL // tr),
        in_specs=[pl.BlockSpec((None, tr, IN_COLS), lambda l, i: (l, i, 0))],
        out_specs=pl.BlockSpec((None, tr, PROJ_COLS), lambda l, i: (l, i, 0)),
        out_shape=jax.ShapeDtypeStruct((DEPTH, D_MODEL, PROJ_COLS), BF16),
        compiler_params=_cparams(("parallel", "parallel")),
        name="pack_w_in",
    )(w_in)


def _inproj_kernel(x_ref, g_ref, w_ref, o_ref, h_ref):
    @pl.when(pl.program_id(1) == 0)
    def _():
        x = x_ref[...]
        ms = jnp.mean(x * x, axis=-1, keepdims=True)
        h_ref[...] = (x * lax.rsqrt(ms + EPS) * g_ref[...]).astype(BF16)

    o_ref[...] = jnp.dot(h_ref[...], w_ref[...], preferred_element_type=F32)


def _inproj(x, gain, w, layer, tm=1024, tn=1024):
    T = x.shape[0]
    tm = min(tm, T)
    return pl.pallas_call(
        _inproj_kernel,
        grid=(T // tm, PROJ_COLS // tn),
        in_specs=[
            pl.BlockSpec((tm, D_MODEL), lambda i, j: (i, 0)),
            pl.BlockSpec((1, D_MODEL), lambda i, j: (0, 0)),
            pl.BlockSpec((None, D_MODEL, tn), lambda i, j: (layer, 0, j)),
        ],
        out_specs=pl.BlockSpec((tm, tn), lambda i, j: (i, j)),
        out_shape=jax.ShapeDtypeStruct((T, PROJ_COLS), F32),
        scratch_shapes=[pltpu.VMEM((tm, D_MODEL), BF16)],
        compiler_params=_cparams(("parallel", "arbitrary")),
        name="inproj",
    )(x, gain, w)


def _hgrn_kernel(*refs, reverse, tb):
    if reverse:
        q_ref, z_ref, v_ref, lb_ref, ofwd_ref, gate_ref, gain_ref, o_ref, state_ref, prev_ref, pair_ref = refs
    else:
        q_ref, z_ref, v_ref, lb_ref, o_ref, state_ref, prev_ref, pair_ref = refs
    C, c = HGRN_CHUNK, HGRN_SUB
    nsub = C // c
    nch = tb // C
    U = min(HGRN_UNROLL, nch)

    @pl.when(pl.program_id(2) == 0)
    def _():
        state_ref[...] = jnp.zeros_like(state_ref)

    lb = lb_ref[...]
    log_lb = jnp.log(lb) * LOG2E
    log_1m_lb = jnp.log1p(-lb) * LOG2E
    one_m_lb = 1.0 - lb

    row = lax.broadcasted_iota(jnp.int32, (C, C), 0)
    col = lax.broadcasted_iota(jnp.int32, (C, C), 1)
    tri = (col >= row) if reverse else (col <= row)
    tri_b = tri.astype(BF16)

    def gates(off):
        z = z_ref[pl.ds(off, C), :]
        q_raw = q_ref[pl.ds(off, C), :]
        v = v_ref[pl.ds(off, C), :].astype(BF16)

        e = jnp.exp2(jnp.abs(z) * -LOG2E)
        r = 1.0 / (1.0 + e)
        log_sig = jnp.minimum(z, 0.0) * LOG2E - jnp.log2(1.0 + e)
        k = one_m_lb * jnp.where(z >= 0.0, e * r, r)
        t = log_1m_lb + log_sig
        g = jnp.maximum(log_lb, t) + jnp.log2(1.0 + jnp.exp2(-jnp.abs(log_lb - t)))
        q = q_raw * (1.0 / (1.0 + jnp.exp(-q_raw)))

        g_hi = g.astype(BF16)
        rem = g - g_hi.astype(F32)
        g_mid = rem.astype(BF16)
        g_lo = (rem - g_mid.astype(F32)).astype(BF16)
        b3 = jnp.dot(tri_b, jnp.concatenate([g_hi, g_mid, g_lo], axis=1), preferred_element_type=F32)
        return q, k, v, b3

    def sub_chunk_scores(q, k, v, b3, exact):
        b = b3[:, :LANE] + b3[:, LANE:2 * LANE] + b3[:, 2 * LANE:]
        b_last = b[0:1] if reverse else b[C - 1:C]
        rows = []
        worst = jnp.zeros_like(b_last)
        k_bf = k.astype(BF16)
        for i in range(0 if exact else nsub):
            if reverse:
                ref_i = b[(i + 1) * c:(i + 1) * c + 1] if i < nsub - 1 else jnp.zeros_like(b_last)
                end_i = b[i * c:i * c + 1]
            else:
                ref_i = b[i * c - 1:i * c] if i > 0 else jnp.zeros_like(b_last)
                end_i = b[(i + 1) * c - 1:(i + 1) * c]
            worst = jnp.minimum(worst, end_i - ref_i)
            blk = slice(i * c, (i + 1) * c)
            q_i = (q[blk] * jnp.exp2(b[blk] - ref_i)).astype(BF16)
            if reverse:
                k_i = (k[i * c:] * jnp.exp2(ref_i - b[i * c:])).astype(BF16)
                k_i = jnp.concatenate([k_bf[:i * c], k_i], axis=0) if i > 0 else k_i
            else:
                k_i = (k[:(i + 1) * c] * jnp.exp2(ref_i - b[:(i + 1) * c])).astype(BF16)
                k_i = jnp.concatenate([k_i, k_bf[(i + 1) * c:]], axis=0) if i < nsub - 1 else k_i
            rows.append(lax.dot_general(q_i, k_i, (((1,), (1,)), ((), ())), preferred_element_type=F32))
        q_in = (q * jnp.exp2(b)).astype(BF16)
        k_end = (k * jnp.exp2(b_last - b)).astype(BF16)
        return (rows, v), (q, k, b, v), worst, (q_in, k_end, jnp.exp2(b_last), v)

    def intra(rows, v):
        scores = jnp.where(tri, jnp.concatenate(rows, axis=0), 0.0).astype(BF16)
        return jnp.dot(scores, v, preferred_element_type=F32)

    def intra_exact(q, k, b, v):
        pair_ref[0] = b
        pair_ref[1] = k
        pair_ref[2] = v.astype(F32)
        t_idx = lax.broadcasted_iota(jnp.int32, (C, 1), 0)

        def source(s, acc):
            b_s = pair_ref[0, pl.ds(s, 1), :]
            w = q * pair_ref[1, pl.ds(s, 1), :] * jnp.exp2(jnp.minimum(b - b_s, 0.0))
            score = jnp.sum(w, axis=1, keepdims=True)
            seen = (t_idx <= s) if reverse else (t_idx >= s)
            return acc + jnp.where(seen, score, 0.0) * pair_ref[2, pl.ds(s, 1), :]

        return lax.fori_loop(0, C, source, jnp.zeros((C, HEAD_DIM), F32))

    def run_chunks(ci, exact):
        offs = []
        for u in range(U):
            cu = ci * U + u
            offs.append(pl.multiple_of(((nch - 1 - cu) if reverse else cu) * C, C))
        parts = [gates(off) for off in offs]
        parts = [sub_chunk_scores(*p, exact) for p in parts]
        worst = parts[0][2]
        for p in parts[1:]:
            worst = jnp.minimum(worst, p[2])
        o_intras = [intra_exact(*p[1]) if exact else intra(*p[0]) for p in parts]
        adds = [lax.dot_general(v, k_end, (((0,), (0,)), ((), ())), preferred_element_type=F32)
                for _, k_end, _, v in (p[3] for p in parts)]
        state = state_ref[...]
        for off, o_intra, add, (q_in, _, decay, _) in zip(offs, o_intras, adds, (p[3] for p in parts)):
            o = o_intra + lax.dot_general(q_in, state.astype(BF16), (((1,), (1,)), ((), ())),
                                          preferred_element_type=F32)
            state = state * decay + add
            if reverse:
                tot = o + ofwd_ref[pl.ds(off, C), :]
                y = tot * lax.rsqrt(jnp.mean(tot * tot, axis=-1, keepdims=True) + EPS) * gain_ref[...]
                gt = gate_ref[pl.ds(off, C), :]
                o_ref[pl.ds(off, C), :] = (y * gt * (1.0 / (1.0 + jnp.exp(-gt)))).astype(o_ref.dtype)
            else:
                o_ref[pl.ds(off, C), :] = o
        state_ref[...] = state
        return jnp.min(worst)

    def chunks(ci, carry):
        prev_ref[...] = state_ref[...]
        lowest = run_chunks(ci, exact=False)

        @pl.when(jnp.logical_not(lowest >= -HGRN_MAX_SUB_DECAY))
        def _():
            state_ref[...] = prev_ref[...]
            run_chunks(ci, exact=True)

        return carry

    lax.fori_loop(0, nch // U, chunks, 0)


def _hgrn_direction(proj, lb, reverse, ofwd=None, gain=None, tb=2048):
    B, S, _ = proj.shape
    tb = min(tb, S)
    nblk = S // tb
    blk = (lambda j: nblk - 1 - j) if reverse else (lambda j: j)

    def col_spec(col0):
        return pl.BlockSpec((None, tb, LANE), lambda b, h, j: (b, blk(j), col0 // LANE + h))

    z_col = COL_ZB if reverse else COL_ZF
    lb_row = 1 if reverse else 0
    in_specs = [col_spec(COL_Q), col_spec(z_col), col_spec(COL_I),
                pl.BlockSpec((None, 1, LANE), lambda b, h, j: (lb_row, 0, h))]
    args = [proj, proj, proj, lb]
    out_spec = pl.BlockSpec((None, tb, LANE), lambda b, h, j: (b, blk(j), h))
    if reverse:
        in_specs += [col_spec(0), col_spec(COL_GA), pl.BlockSpec((1, LANE), lambda b, h, j: (0, 0))]
        args += [ofwd, proj, gain]
        out_dtype = BF16
    else:
        out_dtype = F32
    return pl.pallas_call(
        functools.partial(_hgrn_kernel, reverse=reverse, tb=tb),
        grid=(B, HEADS, nblk),
        in_specs=in_specs,
        out_specs=out_spec,
        out_shape=jax.ShapeDtypeStruct((B, S, E_HGRN), out_dtype),
        scratch_shapes=[pltpu.VMEM((HEAD_DIM, HEAD_DIM), F32), pltpu.VMEM((HEAD_DIM, HEAD_DIM), F32),
                        pltpu.VMEM((3, HGRN_CHUNK, HEAD_DIM), F32)],
        compiler_params=_cparams(("parallel", "parallel", "arbitrary")),
        name="hgrn_bwd" if reverse else "hgrn_fwd",
    )(*args)


def _mla_prep_kernel(p_ref, gq_ref, gkv_ref, wq_ref, wkv_ref, cos_ref, sin_ref,
                     qt_ref, k_ref, vt_ref):
    def rms(x, gain):
        return (x * lax.rsqrt(jnp.mean(x * x, axis=-1, keepdims=True) + EPS) * gain).astype(BF16)

    blk = p_ref[...]
    cos = cos_ref[...]
    sin = sin_ref[...]
    c_q = rms(blk[:, :Q_LORA], gq_ref[...])
    c_kv = rms(blk[:, Q_LORA:Q_LORA + KV_LORA], gkv_ref[...])
    rope_k = blk[:, Q_LORA + KV_LORA:Q_LORA + KV_LORA + LANE]
    rope_k_swapped = blk[:, Q_LORA + KV_LORA + LANE:]
    k_rot = (rope_k * cos + rope_k_swapped * sin).astype(BF16)

    q_all = jnp.dot(c_q, wq_ref[...], preferred_element_type=F32)
    kv = jnp.dot(c_kv, wkv_ref[...], preferred_element_type=F32)
    for h in range(HEADS):
        base = 3 * LANE * h
        q_nope = q_all[:, base:base + LANE]
        q_rot = q_all[:, base + LANE:base + 2 * LANE] * cos + q_all[:, base + 2 * LANE:base + 3 * LANE] * sin
        q_h = jnp.concatenate([q_nope, q_rot], axis=1) * (ATTN_SCALE * LOG2E)
        qt_ref[h * QK_PAD:(h + 1) * QK_PAD, :] = q_h.T.astype(BF16)
        k_ref[:, h * QK_PAD:h * QK_PAD + LANE] = kv[:, h * LANE:(h + 1) * LANE].astype(BF16)
        k_ref[:, h * QK_PAD + LANE:(h + 1) * QK_PAD] = k_rot
        vt_ref[h, 0] = kv[:, E_MLA + h * LANE:E_MLA + (h + 1) * LANE].T.astype(BF16)


def _mla_prep(proj, gq, gkv, wq, wkv, cos, sin, tm):
    B, S, _ = proj.shape
    ns = S // tm
    full = lambda shape: pl.BlockSpec(shape, lambda b, i: (0,) * len(shape))
    return pl.pallas_call(
        _mla_prep_kernel,
        grid=(B, ns),
        in_specs=[
            pl.BlockSpec((None, tm, 1024), lambda b, i: (b, i, COL_MISC // 1024)),
            full((1, Q_LORA)), full((1, KV_LORA)),
            full((Q_LORA, HEADS * 3 * LANE)), full((KV_LORA, 2 * E_MLA)),
            pl.BlockSpec((tm, LANE), lambda b, i: (i, 0)),
            pl.BlockSpec((tm, LANE), lambda b, i: (i, 0)),
        ],
        out_specs=[
            pl.BlockSpec((None, HEADS * QK_PAD, tm), lambda b, i: (b, 0, i)),
            pl.BlockSpec((None, tm, HEADS * QK_PAD), lambda b, i: (b, i, 0)),
            pl.BlockSpec((None, HEADS, 1, HEAD_DIM, tm), lambda b, i: (b, 0, i, 0, 0)),
        ],
        out_shape=[
            jax.ShapeDtypeStruct((B, HEADS * QK_PAD, S), BF16),
            jax.ShapeDtypeStruct((B, S, HEADS * QK_PAD), BF16),
            jax.ShapeDtypeStruct((B, HEADS, ns, HEAD_DIM, tm), BF16),
        ],
        compiler_params=_cparams(("parallel", "parallel")),
        name="mla_prep",
    )(proj, gq, gkv, wq, wkv, cos, sin)


def _attn_kernel(qt_ref, k_ref, vt_ref, gate_ref, o_ref, kmax_ref, m_ref, l_ref, acc_ref,
                 s0_ref, s1_ref, *, tk):
    nkv = k_ref.shape[0] // tk
    tq = qt_ref.shape[1]
    groups = [slice(g * ATTN_GROUP, (g + 1) * ATTN_GROUP) for g in range(tq // ATTN_GROUP)]

    def key_tile(j):
        return k_ref[pl.ds(pl.multiple_of(j * tk, tk), tk), :]

    @pl.when(pl.program_id(2) == 0)
    def _():
        def tile_max(j, mx):
            k = key_tile(j).astype(F32)
            return jnp.maximum(mx, jnp.max(jnp.sum(k * k, axis=1, keepdims=True), axis=0, keepdims=True))

        mx = lax.fori_loop(0, nkv, tile_max, jnp.zeros((1, 1), F32))
        kmax_ref[...] = jnp.broadcast_to(jnp.sqrt(mx), kmax_ref.shape)

    q = qt_ref[...].astype(F32)
    m_ref[...] = jnp.sqrt(jnp.sum(q * q, axis=0, keepdims=True)) * kmax_ref[:, 0:1]
    l_ref[...] = jnp.zeros_like(l_ref)
    acc_ref[...] = jnp.zeros_like(acc_ref)

    def scores_into(j, s_ref):
        k = key_tile(j)
        for sl in groups:
            s_ref[:, sl] = jnp.dot(k, qt_ref[:, sl], preferred_element_type=F32)

    bufs = (s0_ref, s1_ref)
    n = min(ATTN_UNROLL, nkv)

    def run_tiles(j0, last):
        for u in range(n):
            cur, nxt = bufs[u % 2], bufs[(u + 1) % 2]
            vt = vt_ref[j0 + u]
            k_next = key_tile(j0 + u + 1) if (u < n - 1 or not last) else None
            for sl in groups:
                if k_next is not None:
                    nxt[:, sl] = jnp.dot(k_next, qt_ref[:, sl], preferred_element_type=F32)
                p = jnp.exp2(cur[:, sl] - m_ref[:, sl])
                l_ref[:, sl] += jnp.sum(p, axis=0, keepdims=True)
                acc_ref[:, sl] += jnp.dot(vt, p.astype(BF16), preferred_element_type=F32)

    scores_into(0, s0_ref)

    def body(jj, carry):
        run_tiles(jj * n, False)
        return carry

    lax.fori_loop(0, nkv // n - 1, body, 0)
    run_tiles(nkv - n, True)

    def finish():
        out = (acc_ref[...] / l_ref[...]).T
        gt = gate_ref[...]
        o_ref[...] = (out * gt * (1.0 / (1.0 + jnp.exp(-gt)))).astype(o_ref.dtype)

    finish()

    @pl.when(jnp.logical_not(jnp.min(l_ref[...]) >= ATTN_MIN_ROW_SUM))
    def _():
        m_ref[...] = jnp.full_like(m_ref, -jnp.inf)
        l_ref[...] = jnp.zeros_like(l_ref)
        acc_ref[...] = jnp.zeros_like(acc_ref)

        def online(j, carry):
            scores_into(j, s0_ref)
            vt = vt_ref[j]
            for sl in groups:
                st = s0_ref[:, sl]
                m_prev = m_ref[:, sl]
                m_new = jnp.maximum(m_prev, jnp.max(st, axis=0, keepdims=True))
                alpha = jnp.exp2(m_prev - m_new)
                p = jnp.exp2(st - m_new)
                l_ref[:, sl] = alpha * l_ref[:, sl] + jnp.sum(p, axis=0, keepdims=True)
                acc_ref[:, sl] = alpha * acc_ref[:, sl] + jnp.dot(vt, p.astype(BF16),
                                                                  preferred_element_type=F32)
                m_ref[:, sl] = m_new
            return carry

        lax.fori_loop(0, nkv, online, 0)
        finish()


def _attention(qt, k, vt, proj, tq, tk):
    B, S, _ = k.shape
    return pl.pallas_call(
        functools.partial(_attn_kernel, tk=tk),
        grid=(B, HEADS, S // tq),
        in_specs=[
            pl.BlockSpec((None, QK_PAD, tq), lambda b, h, i: (b, h, i)),
            pl.BlockSpec((None, S, QK_PAD), lambda b, h, i: (b, 0, h)),
            pl.BlockSpec((None, None, S // tk, HEAD_DIM, tk), lambda b, h, i: (b, h, 0, 0, 0)),
            pl.BlockSpec((None, tq, LANE), lambda b, h, i: (b, i, COL_GB // LANE + h)),
        ],
        out_specs=pl.BlockSpec((None, tq, LANE), lambda b, h, i: (b, i, h)),
        out_shape=jax.ShapeDtypeStruct((B, S, E_MLA), BF16),
        scratch_shapes=[pltpu.VMEM((1, LANE), F32), pltpu.VMEM((1, tq), F32), pltpu.VMEM((1, tq), F32),
                        pltpu.VMEM((HEAD_DIM, tq), F32),
                        pltpu.VMEM((tk, tq), F32), pltpu.VMEM((tk, tq), F32)],
        compiler_params=_cparams(("parallel", "parallel", "arbitrary")),
        name="attention",
    )(qt, k, vt, proj)


def _merge_kernel(x_ref, ya_ref, yb_ref, ma_ref, mb_ref, wpa_ref, wpb_ref, wout_ref, gain_ref, o_ref,
                  *, last_layer):
    def sigmoid(t):
        return 1.0 / (1.0 + jnp.exp(-t))

    y_a = jnp.dot(ya_ref[...], wpa_ref[...], preferred_element_type=F32)
    y_b = jnp.dot(yb_ref[...], wpb_ref[...], preferred_element_type=F32)
    merged = sigmoid(ma_ref[...]) * y_a + sigmoid(mb_ref[...]) * y_b
    x = x_ref[...] + jnp.dot(merged.astype(BF16), wout_ref[...], preferred_element_type=F32)
    if last_layer:
        x = x * lax.rsqrt(jnp.mean(x * x, axis=-1, keepdims=True) + EPS) * gain_ref[...]
    o_ref[...] = x


def _merge(x, ya, yb, proj, wpa, wpb, wout, final_gain, last_layer, tm=256):
    T = x.shape[0]
    tm = min(tm, T)
    row = lambda width, colblk: pl.BlockSpec((tm, width), lambda i: (i, colblk))
    full = lambda shape: pl.BlockSpec(shape, lambda i: (0, 0))
    return pl.pallas_call(
        functools.partial(_merge_kernel, last_layer=last_layer),
        grid=(T // tm,),
        in_specs=[row(D_MODEL, 0), row(E_HGRN, 0), row(E_MLA, 0), row(D_MODEL, 0), row(D_MODEL, 1),
                  full((E_HGRN, D_MODEL)), full((E_MLA, D_MODEL)), full((D_MODEL, D_MODEL)),
                  full((1, D_MODEL))],
        out_specs=row(D_MODEL, 0),
        out_shape=jax.ShapeDtypeStruct((T, D_MODEL), F32),
        compiler_params=_cparams(("parallel",)),
        name="merge",
    )(x, ya, yb, proj, proj, wpa, wpb, wout, final_gain)


def _prep_layer_weights(w_uq, w_ukv, w_pa, w_pb, w_out):
    wq = w_uq.reshape(Q_LORA, HEADS, HEAD_DIM + QK_ROPE)
    r1 = wq[:, :, HEAD_DIM:HEAD_DIM + ROPE_HALF]
    r2 = wq[:, :, HEAD_DIM + ROPE_HALF:]
    zq = jnp.zeros((Q_LORA, HEADS, LANE - QK_ROPE), w_uq.dtype)
    wq = jnp.concatenate([wq[:, :, :HEAD_DIM], r1, r2, zq, r2, r1, zq], axis=2)
    wq = wq.reshape(Q_LORA, HEADS * 3 * LANE).astype(BF16)

    wkv = w_ukv.reshape(KV_LORA, HEADS, 2 * HEAD_DIM)
    wkv = jnp.concatenate([wkv[:, :, :HEAD_DIM].reshape(KV_LORA, E_MLA),
                           wkv[:, :, HEAD_DIM:].reshape(KV_LORA, E_MLA)], axis=1).astype(BF16)
    return wq, wkv, w_pa.astype(BF16), w_pb.astype(BF16), w_out.astype(BF16)


def _rope_tables(S):
    inv_freq = ROPE_BASE ** (-jnp.arange(ROPE_HALF, dtype=F32) / ROPE_HALF)
    ang = jnp.arange(S, dtype=F32)[:, None] * inv_freq[None, :]
    cos, sin = jnp.cos(ang), jnp.sin(ang)
    zeros = jnp.zeros((S, LANE - QK_ROPE), F32)
    return (jnp.concatenate([cos, cos, zeros], axis=1),
            jnp.concatenate([-sin, sin, zeros], axis=1))


def _trunk(x, w_proj, layers, lbs, norm_gain, hgrn_norm_gain, q_norm_gain, kv_norm_gain, final_norm_gain):
    B, S, _ = x.shape
    T = B * S
    cos, sin = _rope_tables(S)
    tk = min(512, S // 2)
    tq = min(1024, S)
    xf = x.reshape(T, D_MODEL)
    for l in range(DEPTH):
        wq, wkv, wpa, wpb, wout = layers[l]
        proj = _inproj(xf, norm_gain[l].reshape(1, D_MODEL), w_proj, l)
        proj3 = proj.reshape(B, S, PROJ_COLS)
        lb = lbs[l].reshape(2, 1, E_HGRN)
        o_fwd = _hgrn_direction(proj3, lb, reverse=False)
        ya = _hgrn_direction(proj3, lb, reverse=True, ofwd=o_fwd,
                             gain=hgrn_norm_gain[l].reshape(1, HEAD_DIM))
        qt, k, vt = _mla_prep(proj3, q_norm_gain[l].reshape(1, Q_LORA),
                              kv_norm_gain[l].reshape(1, KV_LORA), wq, wkv, cos, sin, tk)
        yb = _attention(qt, k, vt, proj3, tq, tk)
        xf = _merge(xf, ya.reshape(T, E_HGRN), yb.reshape(T, E_MLA), proj, wpa, wpb, wout,
                    final_norm_gain.reshape(1, D_MODEL), last_layer=(l == DEPTH - 1))
    return xf.reshape(B, S, D_MODEL)


def kernel(x_prompt, x_sample, norm_gain, w_in, lb_logits, hgrn_norm_gain, q_norm_gain, w_uq,
           kv_norm_gain, w_ukv, w_pa, w_pb, w_out, final_norm_gain):
    lbs = _lower_bounds(lb_logits)
    layers = [_prep_layer_weights(w_uq[l], w_ukv[l], w_pa[l], w_pb[l], w_out[l]) for l in range(DEPTH)]
    args = (_pack_w_in(w_in), layers, lbs, norm_gain, hgrn_norm_gain, q_norm_gain, kv_norm_gain,
            final_norm_gain)
    return (_trunk(x_prompt, *args), _trunk(x_sample, *args))
```

```python
import functools

import jax
import jax.numpy as jnp
import numpy as np
from jax import lax
from jax.experimental import pallas as pl
from jax.experimental.pallas import tpu as pltpu

F32 = jnp.float32
BF16 = jnp.bfloat16

D_MODEL = 2048
DEPTH = 4
E_HGRN = 1024
HEADS = 8
HEAD_DIM = 128
QK_ROPE = 64
ROPE_HALF = QK_ROPE // 2
Q_LORA = 512
KV_LORA = 256
E_MLA = 1024
ROPE_BASE = 10000.0
EPS = 1e-6
ATTN_SCALE = (HEAD_DIM + QK_ROPE) ** -0.5

PROJ_COLS = 11264
COL_Q, COL_ZF, COL_ZB, COL_I, COL_GA = 4096, 5120, 6144, 7168, 8192
COL_MISC, COL_GB = 9216, 10240
LANE = 128
QK_PAD = 2 * HEAD_DIM

HGRN_CHUNK = 128
HGRN_SUB = 32
HGRN_UNROLL = 16
HGRN_MAX_SUB_DECAY = 120.0
ATTN_GROUP = 256
ATTN_UNROLL = 4
ATTN_MIN_ROW_SUM = 2.0 ** -80
LOG2E = 1.4426950408889634
VMEM_LIMIT = 56 * 1024 * 1024


def _cparams(sem):
    return pltpu.CompilerParams(dimension_semantics=sem, vmem_limit_bytes=VMEM_LIMIT)


def _lower_bound_kernel(x_ref, o_ref):
    x = x_ref[...]
    e = jnp.exp(x - jnp.max(x, axis=0, keepdims=True))
    p = e / jnp.sum(e, axis=0, keepdims=True)
    c = p[0:1]
    rows = [c]
    for l in range(1, DEPTH):
        c = c + p[l:l + 1]
        rows.append(c)
    o_ref[...] = jnp.concatenate([r - rows[0] for r in rows], axis=0)


def _lower_bounds(lb_logits):
    x = lb_logits.reshape(DEPTH, 2 * E_HGRN)
    return pl.pallas_call(
        _lower_bound_kernel,
        out_shape=jax.ShapeDtypeStruct((DEPTH, 2 * E_HGRN), F32),
        name="lower_bounds",
    )(x)


IN_OFFSETS = tuple(int(v) for v in np.cumsum(
    (0, E_HGRN, E_HGRN, E_HGRN, E_HGRN, E_HGRN, Q_LORA, KV_LORA, QK_ROPE, E_MLA, D_MODEL, D_MODEL)))
IN_COLS = IN_OFFSETS[-1]


def _pack_w_in_kernel(w_ref, o_ref):
    o = IN_OFFSETS

    def put(dst, first, last):
        o_ref[:, dst:dst + o[last] - o[first]] = w_ref[:, o[first]:o[last]].astype(BF16)

    put(0, 9, 11)
    put(COL_Q, 0, 5)
    put(COL_MISC, 5, 7)
    rope = w_ref[:, o[7]:o[8]]
    zeros = jnp.zeros((rope.shape[0], LANE - QK_ROPE), F32)
    rope_at = COL_MISC + Q_LORA + KV_LORA
    o_ref[:, rope_at:rope_at + LANE] = jnp.concatenate([rope, zeros], axis=1).astype(BF16)
    o_ref[:, rope_at + LANE:rope_at + 2 * LANE] = jnp.concatenate(
        [rope[:, ROPE_HALF:], rope[:, :ROPE_HALF], zeros], axis=1).astype(BF16)
    put(COL_GB, 8, 9)


def _pack_w_in(w_in, tr=256):
    return pl.pallas_call(
        _pack_w_in_kernel,
        grid=(DEPTH, D_MODEL // tr),
        in_specs=[pl.BlockSpec((None, tr, IN_COLS), lambda l, i: (l, i, 0))],
        out_specs=pl.BlockSpec((None, tr, PROJ_COLS), lambda l, i: (l, i, 0)),
        out_shape=jax.ShapeDtypeStruct((DEPTH, D_MODEL, PROJ_COLS), BF16),
        compiler_params=_cparams(("parallel", "parallel")),
        name="pack_w_in",
    )(w_in)


def _inproj_kernel(x_ref, g_ref, w_ref, o_ref, h_ref):
    @pl.when(pl.program_id(1) == 0)
    def _():
        x = x_ref[...]
        ms = jnp.mean(x * x, axis=-1, keepdims=True)
        h_ref[...] = (x * lax.rsqrt(ms + EPS) * g_ref[...]).astype(BF16)

    o_ref[...] = jnp.dot(h_ref[...], w_ref[...], preferred_element_type=F32)


def _inproj(x, gain, w, layer, tm=1024, tn=1024):
    T = x.shape[0]
    tm = min(tm, T)
    return pl.pallas_call(
        _inproj_kernel,
        grid=(T // tm, PROJ_COLS // tn),
        in_specs=[
            pl.BlockSpec((tm, D_MODEL), lambda i, j: (i, 0)),
            pl.BlockSpec((1, D_MODEL), lambda i, j: (0, 0)),
            pl.BlockSpec((None, D_MODEL, tn), lambda i, j: (layer, 0, j)),
        ],
        out_specs=pl.BlockSpec((tm, tn), lambda i, j: (i, j)),
        out_shape=jax.ShapeDtypeStruct((T, PROJ_COLS), F32),
        scratch_shapes=[pltpu.VMEM((tm, D_MODEL), BF16)],
        compiler_params=_cparams(("parallel", "arbitrary")),
        name="inproj",
    )(x, gain, w)


def _hgrn_kernel(*refs, reverse, tb):
    if reverse:
        q_ref, z_ref, v_ref, lb_ref, ofwd_ref, gate_ref, gain_ref, o_ref, state_ref, prev_ref, pair_ref = refs
    else:
        q_ref, z_ref, v_ref, lb_ref, o_ref, state_ref, prev_ref, pair_ref = refs
    C, c = HGRN_CHUNK, HGRN_SUB
    nsub = C // c
    nch = tb // C
    U = min(HGRN_UNROLL, nch)

    @pl.when(pl.program_id(2) == 0)
    def _():
        state_ref[...] = jnp.zeros_like(state_ref)

    lb = lb_ref[...]
    log_lb = jnp.log(lb) * LOG2E
    log_1m_lb = jnp.log1p(-lb) * LOG2E
    one_m_lb = 1.0 - lb

    row = lax.broadcasted_iota(jnp.int32, (C, C), 0)
    col = lax.broadcasted_iota(jnp.int32, (C, C), 1)
    tri = (col >= row) if reverse else (col <= row)
    tri_b = tri.astype(BF16)

    def gates(off):
        z = z_ref[pl.ds(off, C), :]
        q_raw = q_ref[pl.ds(off, C), :]
        v = v_ref[pl.ds(off, C), :].astype(BF16)

        e = jnp.exp2(jnp.abs(z) * -LOG2E)
        r = 1.0 / (1.0 + e)
        log_sig = jnp.minimum(z, 0.0) * LOG2E - jnp.log2(1.0 + e)
        k = one_m_lb * jnp.where(z >= 0.0, e * r, r)
        t = log_1m_lb + log_sig
        g = jnp.maximum(log_lb, t) + jnp.log2(1.0 + jnp.exp2(-jnp.abs(log_lb - t)))
        q = q_raw * (1.0 / (1.0 + jnp.exp(-q_raw)))

        g_hi = g.astype(BF16)
        rem = g - g_hi.astype(F32)
        g_mid = rem.astype(BF16)
        g_lo = (rem - g_mid.astype(F32)).astype(BF16)
        b3 = jnp.dot(tri_b, jnp.concatenate([g_hi, g_mid, g_lo], axis=1), preferred_element_type=F32)
        return q, k, v, b3

    def sub_chunk_scores(q, k, v, b3, exact):
        b = b3[:, :LANE] + b3[:, LANE:2 * LANE] + b3[:, 2 * LANE:]
        b_last = b[0:1] if reverse else b[C - 1:C]
        rows = []
        worst = jnp.zeros_like(b_last)
        k_bf = k.astype(BF16)
        for i in range(0 if exact else nsub):
            if reverse:
                ref_i = b[(i + 1) * c:(i + 1) * c + 1] if i < nsub - 1 else jnp.zeros_like(b_last)
                end_i = b[i * c:i * c + 1]
            else:
                ref_i = b[i * c - 1:i * c] if i > 0 else jnp.zeros_like(b_last)
                end_i = b[(i + 1) * c - 1:(i + 1) * c]
            worst = jnp.minimum(worst, end_i - ref_i)
            blk = slice(i * c, (i + 1) * c)
            q_i = (q[blk] * jnp.exp2(b[blk] - ref_i)).astype(BF16)
            if reverse:
                k_i = (k[i * c:] * jnp.exp2(ref_i - b[i * c:])).astype(BF16)
                k_i = jnp.concatenate([k_bf[:i * c], k_i], axis=0) if i > 0 else k_i
            else:
                k_i = (k[:(i + 1) * c] * jnp.exp2(ref_i - b[:(i + 1) * c])).astype(BF16)
                k_i = jnp.concatenate([k_i, k_bf[(i + 1) * c:]], axis=0) if i < nsub - 1 else k_i
            rows.append(lax.dot_general(q_i, k_i, (((1,), (1,)), ((), ())), preferred_element_type=F32))
        q_in = (q * jnp.exp2(b)).astype(BF16)
        k_end = (k * jnp.exp2(b_last - b)).astype(BF16)
        return (rows, v), (q, k, b, v), worst, (q_in, k_end, jnp.exp2(b_last), v)

    def intra(rows, v):
        scores = jnp.where(tri, jnp.concatenate(rows, axis=0), 0.0).astype(BF16)
        return jnp.dot(scores, v, preferred_element_type=F32)

    def intra_exact(q, k, b, v):
        pair_ref[0] = b
        pair_ref[1] = k
        pair_ref[2] = v.astype(F32)
        t_idx = lax.broadcasted_iota(jnp.int32, (C, 1), 0)

        def source(s, acc):
            b_s = pair_ref[0, pl.ds(s, 1), :]
            w = q * pair_ref[1, pl.ds(s, 1), :] * jnp.exp2(jnp.minimum(b - b_s, 0.0))
            score = jnp.sum(w, axis=1, keepdims=True)
            seen = (t_idx <= s) if reverse else (t_idx >= s)
            return acc + jnp.where(seen, score, 0.0) * pair_ref[2, pl.ds(s, 1), :]

        return lax.fori_loop(0, C, source, jnp.zeros((C, HEAD_DIM), F32))

    def run_chunks(ci, exact):
        offs = []
        for u in range(U):
            cu = ci * U + u
            offs.append(pl.multiple_of(((nch - 1 - cu) if reverse else cu) * C, C))
        parts = [gates(off) for off in offs]
        parts = [sub_chunk_scores(*p, exact) for p in parts]
        worst = parts[0][2]
        for p in parts[1:]:
            worst = jnp.minimum(worst, p[2])
        o_intras = [intra_exact(*p[1]) if exact else intra(*p[0]) for p in parts]
        adds = [lax.dot_general(v, k_end, (((0,), (0,)), ((), ())), preferred_element_type=F32)
                for _, k_end, _, v in (p[3] for p in parts)]
        state = state_ref[...]
        for off, o_intra, add, (q_in, _, decay, _) in zip(offs, o_intras, adds, (p[3] for p in parts)):
            o = o_intra + lax.dot_general(q_in, state.astype(BF16), (((1,), (1,)), ((), ())),
                                          preferred_element_type=F32)
            state = state * decay + add
            if reverse:
                tot = o + ofwd_ref[pl.ds(off, C), :]
                y = tot * lax.rsqrt(jnp.mean(tot * tot, axis=-1, keepdims=True) + EPS) * gain_ref[...]
                gt = gate_ref[pl.ds(off, C), :]
                o_ref[pl.ds(off, C), :] = (y * gt * (1.0 / (1.0 + jnp.exp(-gt)))).astype(o_ref.dtype)
            else:
                o_ref[pl.ds(off, C), :] = o
        state_ref[...] = state
        return jnp.min(worst)

    def chunks(ci, carry):
        prev_ref[...] = state_ref[...]
        lowest = run_chunks(ci, exact=False)

        @pl.when(jnp.logical_not(lowest >= -HGRN_MAX_SUB_DECAY))
        def _():
            state_ref[...] = prev_ref[...]
            run_chunks(ci, exact=True)

        return carry

    lax.fori_loop(0, nch // U, chunks, 0)


def _hgrn_direction(proj, lb, reverse, ofwd=None, gain=None, tb=2048):
    B, S, _ = proj.shape
    tb = min(tb, S)
    nblk = S // tb
    blk = (lambda j: nblk - 1 - j) if reverse else (lambda j: j)

    def col_spec(col0):
        return pl.BlockSpec((None, tb, LANE), lambda b, h, j: (b, blk(j), col0 // LANE + h))

    z_col = COL_ZB if reverse else COL_ZF
    lb_row = 1 if reverse else 0
    in_specs = [col_spec(COL_Q), col_spec(z_col), col_spec(COL_I),
                pl.BlockSpec((None, 1, LANE), lambda b, h, j: (lb_row, 0, h))]
    args = [proj, proj, proj, lb]
    out_spec = pl.BlockSpec((None, tb, LANE), lambda b, h, j: (b, blk(j), h))
    if reverse:
        in_specs += [col_spec(0), col_spec(COL_GA), pl.BlockSpec((1, LANE), lambda b, h, j: (0, 0))]
        args += [ofwd, proj, gain]
        out_dtype = BF16
    else:
        out_dtype = F32
    return pl.pallas_call(
        functools.partial(_hgrn_kernel, reverse=reverse, tb=tb),
        grid=(B, HEADS, nblk),
        in_specs=in_specs,
        out_specs=out_spec,
        out_shape=jax.ShapeDtypeStruct((B, S, E_HGRN), out_dtype),
        scratch_shapes=[pltpu.VMEM((HEAD_DIM, HEAD_DIM), F32), pltpu.VMEM((HEAD_DIM, HEAD_DIM), F32),
                        pltpu.VMEM((3, HGRN_CHUNK, HEAD_DIM), F32)],
        compiler_params=_cparams(("parallel", "parallel", "arbitrary")),
        name="hgrn_bwd" if reverse else "hgrn_fwd",
    )(*args)


def _mla_prep_kernel(p_ref, gq_ref, gkv_ref, wq_ref, wkv_ref, cos_ref, sin_ref,
                     qt_ref, k_ref, vt_ref, knorm_ref):
    def rms(x, gain):
        return (x * lax.rsqrt(jnp.mean(x * x, axis=-1, keepdims=True) + EPS) * gain).astype(BF16)

    blk = p_ref[...]
    cos = cos_ref[...]
    sin = sin_ref[...]
    c_q = rms(blk[:, :Q_LORA], gq_ref[...])
    c_kv = rms(blk[:, Q_LORA:Q_LORA + KV_LORA], gkv_ref[...])
    rope_k = blk[:, Q_LORA + KV_LORA:Q_LORA + KV_LORA + LANE]
    rope_k_swapped = blk[:, Q_LORA + KV_LORA + LANE:]
    k_rot = (rope_k * cos + rope_k_swapped * sin).astype(BF16)

    def max_abs(x):
        return jnp.max(jnp.max(jnp.abs(x.astype(F32)), axis=0, keepdims=True), axis=1, keepdims=True)

    rot_max = max_abs(k_rot)

    q_all = jnp.dot(c_q, wq_ref[...], preferred_element_type=F32)
    kv = jnp.dot(c_kv, wkv_ref[...], preferred_element_type=F32)
    for h in range(HEADS):
        base = 3 * LANE * h
        q_nope = q_all[:, base:base + LANE]
        q_rot = q_all[:, base + LANE:base + 2 * LANE] * cos + q_all[:, base + 2 * LANE:base + 3 * LANE] * sin
        q_h = jnp.concatenate([q_nope, q_rot], axis=1) * (ATTN_SCALE * LOG2E)
        qt_ref[h * QK_PAD:(h + 1) * QK_PAD, :] = q_h.T.astype(BF16)
        k_nope = kv[:, h * LANE:(h + 1) * LANE].astype(BF16)
        k_ref[:, h * QK_PAD:h * QK_PAD + LANE] = k_nope
        k_ref[:, h * QK_PAD + LANE:(h + 1) * QK_PAD] = k_rot
        nope_max = max_abs(k_nope)
        knorm_ref[h:h + 1, :] = jnp.broadcast_to(
            HEAD_DIM * nope_max * nope_max + QK_ROPE * rot_max * rot_max, (1, LANE))
        vt_ref[h, 0] = kv[:, E_MLA + h * LANE:E_MLA + (h + 1) * LANE].T.astype(BF16)


def _mla_prep(proj, gq, gkv, wq, wkv, cos, sin, tm):
    B, S, _ = proj.shape
    ns = S // tm
    full = lambda shape: pl.BlockSpec(shape, lambda b, i: (0,) * len(shape))
    return pl.pallas_call(
        _mla_prep_kernel,
        grid=(B, ns),
        in_specs=[
            pl.BlockSpec((None, tm, 1024), lambda b, i: (b, i, COL_MISC // 1024)),
            full((1, Q_LORA)), full((1, KV_LORA)),
            full((Q_LORA, HEADS * 3 * LANE)), full((KV_LORA, 2 * E_MLA)),
            pl.BlockSpec((tm, LANE), lambda b, i: (i, 0)),
            pl.BlockSpec((tm, LANE), lambda b, i: (i, 0)),
        ],
        out_specs=[
            pl.BlockSpec((None, HEADS * QK_PAD, tm), lambda b, i: (b, 0, i)),
            pl.BlockSpec((None, tm, HEADS * QK_PAD), lambda b, i: (b, i, 0)),
            pl.BlockSpec((None, HEADS, 1, HEAD_DIM, tm), lambda b, i: (b, 0, i, 0, 0)),
            pl.BlockSpec((None, None, HEADS, LANE), lambda b, i: (b, i, 0, 0)),
        ],
        out_shape=[
            jax.ShapeDtypeStruct((B, HEADS * QK_PAD, S), BF16),
            jax.ShapeDtypeStruct((B, S, HEADS * QK_PAD), BF16),
            jax.ShapeDtypeStruct((B, HEADS, ns, HEAD_DIM, tm), BF16),
            jax.ShapeDtypeStruct((B, ns, HEADS, LANE), F32),
        ],
        compiler_params=_cparams(("parallel", "parallel")),
        name="mla_prep",
    )(proj, gq, gkv, wq, wkv, cos, sin)


def _attn_kernel(qt_ref, k_ref, vt_ref, knorm_ref, gate_ref, o_ref, m_ref, l_ref, acc_ref,
                 s0_ref, s1_ref, *, tk):
    nkv = k_ref.shape[0] // tk
    tq = qt_ref.shape[1]
    groups = [slice(g * ATTN_GROUP, (g + 1) * ATTN_GROUP) for g in range(tq // ATTN_GROUP)]

    def key_tile(j):
        return k_ref[pl.ds(pl.multiple_of(j * tk, tk), tk), :]

    k_max = jnp.sqrt(jnp.max(knorm_ref[...], axis=0, keepdims=True))[:, 0:1]
    q = qt_ref[...].astype(F32)
    m_ref[...] = jnp.sqrt(jnp.sum(q * q, axis=0, keepdims=True)) * k_max
    l_ref[...] = jnp.zeros_like(l_ref)
    acc_ref[...] = jnp.zeros_like(acc_ref)

    def scores_into(j, s_ref):
        k = key_tile(j)
        for sl in groups:
            s_ref[:, sl] = jnp.dot(k, qt_ref[:, sl], preferred_element_type=F32)

    bufs = (s0_ref, s1_ref)
    n = min(ATTN_UNROLL, nkv)

    def run_tiles(j0, last):
        for u in range(n):
            cur, nxt = bufs[u % 2], bufs[(u + 1) % 2]
            vt = vt_ref[j0 + u]
            k_next = key_tile(j0 + u + 1) if (u < n - 1 or not last) else None
            for sl in groups:
                if k_next is not None:
                    nxt[:, sl] = jnp.dot(k_next, qt_ref[:, sl], preferred_element_type=F32)
                p = jnp.exp2(cur[:, sl] - m_ref[:, sl])
                l_ref[:, sl] += jnp.sum(p, axis=0, keepdims=True)
                acc_ref[:, sl] += jnp.dot(vt, p.astype(BF16), preferred_element_type=F32)

    scores_into(0, s0_ref)

    def body(jj, carry):
        run_tiles(jj * n, False)
        return carry

    lax.fori_loop(0, nkv // n - 1, body, 0)
    run_tiles(nkv - n, True)

    def finish():
        out = (acc_ref[...] / l_ref[...]).T
        gt = gate_ref[...]
        o_ref[...] = (out * gt * (1.0 / (1.0 + jnp.exp(-gt)))).astype(o_ref.dtype)

    finish()

    @pl.when(jnp.logical_not(jnp.min(l_ref[...]) >= ATTN_MIN_ROW_SUM))
    def _():
        m_ref[...] = jnp.full_like(m_ref, -jnp.inf)
        l_ref[...] = jnp.zeros_like(l_ref)
        acc_ref[...] = jnp.zeros_like(acc_ref)

        def online(j, carry):
            scores_into(j, s0_ref)
            vt = vt_ref[j]
            for sl in groups:
                st = s0_ref[:, sl]
                m_prev = m_ref[:, sl]
                m_new = jnp.maximum(m_prev, jnp.max(st, axis=0, keepdims=True))
                alpha = jnp.exp2(m_prev - m_new)
                p = jnp.exp2(st - m_new)
                l_ref[:, sl] = alpha * l_ref[:, sl] + jnp.sum(p, axis=0, keepdims=True)
                acc_ref[:, sl] = alpha * acc_ref[:, sl] + jnp.dot(vt, p.astype(BF16),
                                                                  preferred_element_type=F32)
                m_ref[:, sl] = m_new
            return carry

        lax.fori_loop(0, nkv, online, 0)
        finish()


def _attention(qt, k, vt, knorm, proj, tq, tk):
    B, S, _ = k.shape
    return pl.pallas_call(
        functools.partial(_attn_kernel, tk=tk),
        grid=(B, HEADS, S // tq),
        in_specs=[
            pl.BlockSpec((None, QK_PAD, tq), lambda b, h, i: (b, h, i)),
            pl.BlockSpec((None, S, QK_PAD), lambda b, h, i: (b, 0, h)),
            pl.BlockSpec((None, None, S // tk, HEAD_DIM, tk), lambda b, h, i: (b, h, 0, 0, 0)),
            pl.BlockSpec((None, None, S // tk, LANE), lambda b, h, i: (b, h, 0, 0)),
            pl.BlockSpec((None, tq, LANE), lambda b, h, i: (b, i, COL_GB // LANE + h)),
        ],
        out_specs=pl.BlockSpec((None, tq, LANE), lambda b, h, i: (b, i, h)),
        out_shape=jax.ShapeDtypeStruct((B, S, E_MLA), BF16),
        scratch_shapes=[pltpu.VMEM((1, tq), F32), pltpu.VMEM((1, tq), F32),
                        pltpu.VMEM((HEAD_DIM, tq), F32),
                        pltpu.VMEM((tk, tq), F32), pltpu.VMEM((tk, tq), F32)],
        compiler_params=_cparams(("parallel", "parallel", "arbitrary")),
        name="attention",
    )(qt, k, vt, knorm, proj)


def _merge_kernel(x_ref, ya_ref, yb_ref, ma_ref, mb_ref, wpa_ref, wpb_ref, wout_ref, gain_ref, o_ref,
                  *, last_layer):
    def sigmoid(t):
        return 1.0 / (1.0 + jnp.exp(-t))

    y_a = jnp.dot(ya_ref[...], wpa_ref[...], preferred_element_type=F32)
    y_b = jnp.dot(yb_ref[...], wpb_ref[...], preferred_element_type=F32)
    merged = sigmoid(ma_ref[...]) * y_a + sigmoid(mb_ref[...]) * y_b
    x = x_ref[...] + jnp.dot(merged.astype(BF16), wout_ref[...], preferred_element_type=F32)
    if last_layer:
        x = x * lax.rsqrt(jnp.mean(x * x, axis=-1, keepdims=True) + EPS) * gain_ref[...]
    o_ref[...] = x


def _merge(x, ya, yb, proj, wpa, wpb, wout, final_gain, last_layer, tm=256):
    T = x.shape[0]
    tm = min(tm, T)
    row = lambda width, colblk: pl.BlockSpec((tm, width), lambda i: (i, colblk))
    full = lambda shape: pl.BlockSpec(shape, lambda i: (0, 0))
    return pl.pallas_call(
        functools.partial(_merge_kernel, last_layer=last_layer),
        grid=(T // tm,),
        in_specs=[row(D_MODEL, 0), row(E_HGRN, 0), row(E_MLA, 0), row(D_MODEL, 0), row(D_MODEL, 1),
                  full((E_HGRN, D_MODEL)), full((E_MLA, D_MODEL)), full((D_MODEL, D_MODEL)),
                  full((1, D_MODEL))],
        out_specs=row(D_MODEL, 0),
        out_shape=jax.ShapeDtypeStruct((T, D_MODEL), F32),
        compiler_params=_cparams(("parallel",)),
        name="merge",
    )(x, ya, yb, proj, proj, wpa, wpb, wout, final_gain)


def _prep_layer_weights(w_uq, w_ukv, w_pa, w_pb, w_out):
    wq = w_uq.reshape(Q_LORA, HEADS, HEAD_DIM + QK_ROPE)
    r1 = wq[:, :, HEAD_DIM:HEAD_DIM + ROPE_HALF]
    r2 = wq[:, :, HEAD_DIM + ROPE_HALF:]
    zq = jnp.zeros((Q_LORA, HEADS, LANE - QK_ROPE), w_uq.dtype)
    wq = jnp.concatenate([wq[:, :, :HEAD_DIM], r1, r2, zq, r2, r1, zq], axis=2)
    wq = wq.reshape(Q_LORA, HEADS * 3 * LANE).astype(BF16)

    wkv = w_ukv.reshape(KV_LORA, HEADS, 2 * HEAD_DIM)
    wkv = jnp.concatenate([wkv[:, :, :HEAD_DIM].reshape(KV_LORA, E_MLA),
                           wkv[:, :, HEAD_DIM:].reshape(KV_LORA, E_MLA)], axis=1).astype(BF16)
    return wq, wkv, w_pa.astype(BF16), w_pb.astype(BF16), w_out.astype(BF16)


def _rope_tables(S):
    inv_freq = ROPE_BASE ** (-jnp.arange(ROPE_HALF, dtype=F32) / ROPE_HALF)
    ang = jnp.arange(S, dtype=F32)[:, None] * inv_freq[None, :]
    cos, sin = jnp.cos(ang), jnp.sin(ang)
    zeros = jnp.zeros((S, LANE - QK_ROPE), F32)
    return (jnp.concatenate([cos, cos, zeros], axis=1),
            jnp.concatenate([-sin, sin, zeros], axis=1))


def _trunk(x, w_proj, layers, lbs, norm_gain, hgrn_norm_gain, q_norm_gain, kv_norm_gain, final_norm_gain):
    B, S, _ = x.shape
    T = B * S
    cos, sin = _rope_tables(S)
    tk = min(512, S // 2)
    tq = min(1024, S)
    xf = x.reshape(T, D_MODEL)
    for l in range(DEPTH):
        wq, wkv, wpa, wpb, wout = layers[l]
        proj = _inproj(xf, norm_gain[l].reshape(1, D_MODEL), w_proj, l)
        proj3 = proj.reshape(B, S, PROJ_COLS)
        lb = lbs[l].reshape(2, 1, E_HGRN)
        o_fwd = _hgrn_direction(proj3, lb, reverse=False)
        ya = _hgrn_direction(proj3, lb, reverse=True, ofwd=o_fwd,
                             gain=hgrn_norm_gain[l].reshape(1, HEAD_DIM))
        qt, k, vt, knorm = _mla_prep(proj3, q_norm_gain[l].reshape(1, Q_LORA),
                              kv_norm_gain[l].reshape(1, KV_LORA), wq, wkv, cos, sin, tk)
        yb = _attention(qt, k, vt, knorm.transpose(0, 2, 1, 3), proj3, tq, tk)
        xf = _merge(xf, ya.reshape(T, E_HGRN), yb.reshape(T, E_MLA), proj, wpa, wpb, wout,
                    final_norm_gain.reshape(1, D_MODEL), last_layer=(l == DEPTH - 1))
    return xf.reshape(B, S, D_MODEL)


def kernel(x_prompt, x_sample, norm_gain, w_in, lb_logits, hgrn_norm_gain, q_norm_gain, w_uq,
           kv_norm_gain, w_ukv, w_pa, w_pb, w_out, final_norm_gain):
    lbs = _lower_bounds(lb_logits)
    layers = [_prep_layer_weights(w_uq[l], w_ukv[l], w_pa[l], w_pb[l], w_out[l]) for l in range(DEPTH)]
    args = (_pack_w_in(w_in), layers, lbs, norm_gain, hgrn_norm_gain, q_norm_gain, kv_norm_gain,
            final_norm_gain)
    return (_trunk(x_prompt, *args), _trunk(x_sample, *args))
```

```python
import functools

import jax
import jax.numpy as jnp
import numpy as np
from jax import lax
from jax.experimental import pallas as pl
from jax.experimental.pallas import tpu as pltpu

F32 = jnp.float32
BF16 = jnp.bfloat16

D_MODEL = 2048
DEPTH = 4
E_HGRN = 1024
HEADS = 8
HEAD_DIM = 128
QK_ROPE = 64
ROPE_HALF = QK_ROPE // 2
Q_LORA = 512
KV_LORA = 256
E_MLA = 1024
ROPE_BASE = 10000.0
EPS = 1e-6
ATTN_SCALE = (HEAD_DIM + QK_ROPE) ** -0.5

PROJ_COLS = 11264
COL_Q, COL_ZF, COL_ZB, COL_I, COL_GA = 4096, 5120, 6144, 7168, 8192
COL_MISC, COL_GB = 9216, 10240
LANE = 128
QK_PAD = 2 * HEAD_DIM

HGRN_CHUNK = 128
HGRN_SUB = 32
HGRN_UNROLL = 16
HGRN_MAX_SUB_DECAY = 120.0
ATTN_GROUP = 256
ATTN_UNROLL = 4
ATTN_MIN_ROW_SUM = 2.0 ** -80
LOG2E = 1.4426950408889634
VMEM_LIMIT = 56 * 1024 * 1024


def _cparams(sem):
    return pltpu.CompilerParams(dimension_semantics=sem, vmem_limit_bytes=VMEM_LIMIT)


def _lower_bound_kernel(x_ref, o_ref):
    x = x_ref[...]
    e = jnp.exp(x - jnp.max(x, axis=0, keepdims=True))
    p = e / jnp.sum(e, axis=0, keepdims=True)
    c = p[0:1]
    rows = [c]
    for l in range(1, DEPTH):
        c = c + p[l:l + 1]
        rows.append(c)
    o_ref[...] = jnp.concatenate([r - rows[0] for r in rows], axis=0)


def _lower_bounds(lb_logits):
    x = lb_logits.reshape(DEPTH, 2 * E_HGRN)
    return pl.pallas_call(
        _lower_bound_kernel,
        out_shape=jax.ShapeDtypeStruct((DEPTH, 2 * E_HGRN), F32),
        name="lower_bounds",
    )(x)


IN_OFFSETS = tuple(int(v) for v in np.cumsum(
    (0, E_HGRN, E_HGRN, E_HGRN, E_HGRN, E_HGRN, Q_LORA, KV_LORA, QK_ROPE, E_MLA, D_MODEL, D_MODEL)))
IN_COLS = IN_OFFSETS[-1]


def _pack_w_in_kernel(w_ref, o_ref):
    o = IN_OFFSETS

    def put(dst, first, last):
        o_ref[:, dst:dst + o[last] - o[first]] = w_ref[:, o[first]:o[last]].astype(BF16)

    put(0, 9, 11)
    put(COL_Q, 0, 5)
    put(COL_MISC, 5, 7)
    rope = w_ref[:, o[7]:o[8]]
    zeros = jnp.zeros((rope.shape[0], LANE - QK_ROPE), F32)
    rope_at = COL_MISC + Q_LORA + KV_LORA
    o_ref[:, rope_at:rope_at + LANE] = jnp.concatenate([rope, zeros], axis=1).astype(BF16)
    o_ref[:, rope_at + LANE:rope_at + 2 * LANE] = jnp.concatenate(
        [rope[:, ROPE_HALF:], rope[:, :ROPE_HALF], zeros], axis=1).astype(BF16)
    put(COL_GB, 8, 9)


def _pack_w_in(w_in, tr=256):
    return pl.pallas_call(
        _pack_w_in_kernel,
        grid=(DEPTH, D_MODEL // tr),
        in_specs=[pl.BlockSpec((None, tr, IN_COLS), lambda l, i: (l, i, 0))],
        out_specs=pl.BlockSpec((None, tr, PROJ_COLS), lambda l, i: (l, i, 0)),
        out_shape=jax.ShapeDtypeStruct((DEPTH, D_MODEL, PROJ_COLS), BF16),
        compiler_params=_cparams(("parallel", "parallel")),
        name="pack_w_in",
    )(w_in)


def _inproj_kernel(x_ref, g_ref, w_ref, o_ref, h_ref):
    @pl.when(pl.program_id(1) == 0)
    def _():
        x = x_ref[...]
        ms = jnp.mean(x * x, axis=-1, keepdims=True)
        h_ref[...] = (x * lax.rsqrt(ms + EPS) * g_ref[...]).astype(BF16)

    o_ref[...] = jnp.dot(h_ref[...], w_ref[...], preferred_element_type=F32)


def _inproj(x, gain, w, layer, tm=1024, tn=1024):
    T = x.shape[0]
    tm = min(tm, T)
    return pl.pallas_call(
        _inproj_kernel,
        grid=(T // tm, PROJ_COLS // tn),
        in_specs=[
            pl.BlockSpec((tm, D_MODEL), lambda i, j: (i, 0)),
            pl.BlockSpec((1, D_MODEL), lambda i, j: (0, 0)),
            pl.BlockSpec((None, D_MODEL, tn), lambda i, j: (layer, 0, j)),
        ],
        out_specs=pl.BlockSpec((tm, tn), lambda i, j: (i, j)),
        out_shape=jax.ShapeDtypeStruct((T, PROJ_COLS), F32),
        scratch_shapes=[pltpu.VMEM((tm, D_MODEL), BF16)],
        compiler_params=_cparams(("parallel", "arbitrary")),
        name="inproj",
    )(x, gain, w)


def _hgrn_kernel(*refs, reverse, tb):
    if reverse:
        q_ref, z_ref, v_ref, lb_ref, ofwd_ref, gate_ref, gain_ref, o_ref, state_ref, prev_ref, pair_ref = refs
    else:
        q_ref, z_ref, v_ref, lb_ref, o_ref, state_ref, prev_ref, pair_ref = refs
    C, c = HGRN_CHUNK, HGRN_SUB
    nsub = C // c
    nch = tb // C
    U = min(HGRN_UNROLL, nch)

    @pl.when(pl.program_id(2) == 0)
    def _():
        state_ref[...] = jnp.zeros_like(state_ref)

    lb = lb_ref[...]
    log_lb = jnp.log(lb) * LOG2E
    log_1m_lb = jnp.log1p(-lb) * LOG2E
    one_m_lb = 1.0 - lb

    row = lax.broadcasted_iota(jnp.int32, (C, C), 0)
    col = lax.broadcasted_iota(jnp.int32, (C, C), 1)
    tri = (col >= row) if reverse else (col <= row)
    tri_b = tri.astype(BF16)

    def gates(off):
        z = z_ref[pl.ds(off, C), :]
        q_raw = q_ref[pl.ds(off, C), :]
        v = v_ref[pl.ds(off, C), :].astype(BF16)

        e = jnp.exp2(jnp.abs(z) * -LOG2E)
        r = 1.0 / (1.0 + e)
        log_sig = jnp.minimum(z, 0.0) * LOG2E - jnp.log2(1.0 + e)
        k = one_m_lb * jnp.where(z >= 0.0, e * r, r)
        t = log_1m_lb + log_sig
        g = jnp.maximum(log_lb, t) + jnp.log2(1.0 + jnp.exp2(-jnp.abs(log_lb - t)))
        q = q_raw * (1.0 / (1.0 + jnp.exp(-q_raw)))

        g_hi = g.astype(BF16)
        rem = g - g_hi.astype(F32)
        g_mid = rem.astype(BF16)
        g_lo = (rem - g_mid.astype(F32)).astype(BF16)
        b3 = jnp.dot(tri_b, jnp.concatenate([g_hi, g_mid, g_lo], axis=1), preferred_element_type=F32)
        return q, k, v, b3

    def sub_chunk_scores(q, k, v, b3, exact):
        b = b3[:, :LANE] + b3[:, LANE:2 * LANE] + b3[:, 2 * LANE:]
        b_last = b[0:1] if reverse else b[C - 1:C]
        rows = []
        worst = jnp.zeros_like(b_last)
        k_bf = k.astype(BF16)
        for i in range(0 if exact else nsub):
            if reverse:
                ref_i = b[(i + 1) * c:(i + 1) * c + 1] if i < nsub - 1 else jnp.zeros_like(b_last)
                end_i = b[i * c:i * c + 1]
            else:
                ref_i = b[i * c - 1:i * c] if i > 0 else jnp.zeros_like(b_last)
                end_i = b[(i + 1) * c - 1:(i + 1) * c]
            worst = jnp.minimum(worst, end_i - ref_i)
            blk = slice(i * c, (i + 1) * c)
            q_i = (q[blk] * jnp.exp2(b[blk] - ref_i)).astype(BF16)
            if reverse:
                k_i = (k[i * c:] * jnp.exp2(ref_i - b[i * c:])).astype(BF16)
                k_i = jnp.concatenate([k_bf[:i * c], k_i], axis=0) if i > 0 else k_i
            else:
                k_i = (k[:(i + 1) * c] * jnp.exp2(ref_i - b[:(i + 1) * c])).astype(BF16)
                k_i = jnp.concatenate([k_i, k_bf[(i + 1) * c:]], axis=0) if i < nsub - 1 else k_i
            rows.append(lax.dot_general(q_i, k_i, (((1,), (1,)), ((), ())), preferred_element_type=F32))
        q_in = (q * jnp.exp2(b)).astype(BF16)
        k_end = (k * jnp.exp2(b_last - b)).astype(BF16)
        return (rows, v), (q, k, b, v), worst, (q_in, k_end, jnp.exp2(b_last), v)

    def intra(rows, v):
        scores = jnp.where(tri, jnp.concatenate(rows, axis=0), 0.0).astype(BF16)
        return jnp.dot(scores, v, preferred_element_type=F32)

    def intra_exact(q, k, b, v):
        pair_ref[0] = b
        pair_ref[1] = k
        pair_ref[2] = v.astype(F32)
        t_idx = lax.broadcasted_iota(jnp.int32, (C, 1), 0)

        def source(s, acc):
            b_s = pair_ref[0, pl.ds(s, 1), :]
            w = q * pair_ref[1, pl.ds(s, 1), :] * jnp.exp2(jnp.minimum(b - b_s, 0.0))
            score = jnp.sum(w, axis=1, keepdims=True)
            seen = (t_idx <= s) if reverse else (t_idx >= s)
            return acc + jnp.where(seen, score, 0.0) * pair_ref[2, pl.ds(s, 1), :]

        return lax.fori_loop(0, C, source, jnp.zeros((C, HEAD_DIM), F32))

    def run_chunks(ci, exact):
        offs = []
        for u in range(U):
            cu = ci * U + u
            offs.append(pl.multiple_of(((nch - 1 - cu) if reverse else cu) * C, C))
        parts = [gates(off) for off in offs]
        parts = [sub_chunk_scores(*p, exact) for p in parts]
        worst = parts[0][2]
        for p in parts[1:]:
            worst = jnp.minimum(worst, p[2])
        o_intras = [intra_exact(*p[1]) if exact else intra(*p[0]) for p in parts]
        adds = [lax.dot_general(v, k_end, (((0,), (0,)), ((), ())), preferred_element_type=F32)
                for _, k_end, _, v in (p[3] for p in parts)]
        state = state_ref[...]
        for off, o_intra, add, (q_in, _, decay, _) in zip(offs, o_intras, adds, (p[3] for p in parts)):
            o = o_intra + lax.dot_general(q_in, state.astype(BF16), (((1,), (1,)), ((), ())),
                                          preferred_element_type=F32)
            state = state * decay + add
            if reverse:
                tot = o + ofwd_ref[pl.ds(off, C), :]
                y = tot * lax.rsqrt(jnp.mean(tot * tot, axis=-1, keepdims=True) + EPS) * gain_ref[...]
                gt = gate_ref[pl.ds(off, C), :]
                o_ref[pl.ds(off, C), :] = (y * gt * (1.0 / (1.0 + jnp.exp(-gt)))).astype(o_ref.dtype)
            else:
                o_ref[pl.ds(off, C), :] = o
        state_ref[...] = state
        return jnp.min(worst)

    def chunks(ci, carry):
        prev_ref[...] = state_ref[...]
        lowest = run_chunks(ci, exact=False)

        @pl.when(jnp.logical_not(lowest >= -HGRN_MAX_SUB_DECAY))
        def _():
            state_ref[...] = prev_ref[...]
            run_chunks(ci, exact=True)

        return carry

    lax.fori_loop(0, nch // U, chunks, 0)


def _hgrn_direction(proj, lb, reverse, ofwd=None, gain=None, tb=2048):
    B, S, _ = proj.shape
    tb = min(tb, S)
    nblk = S // tb
    blk = (lambda j: nblk - 1 - j) if reverse else (lambda j: j)

    def col_spec(col0):
        return pl.BlockSpec((None, tb, LANE), lambda b, h, j: (b, blk(j), col0 // LANE + h))

    z_col = COL_ZB if reverse else COL_ZF
    lb_row = 1 if reverse else 0
    in_specs = [col_spec(COL_Q), col_spec(z_col), col_spec(COL_I),
                pl.BlockSpec((None, 1, LANE), lambda b, h, j: (lb_row, 0, h))]
    args = [proj, proj, proj, lb]
    out_spec = pl.BlockSpec((None, tb, LANE), lambda b, h, j: (b, blk(j), h))
    if reverse:
        in_specs += [col_spec(0), col_spec(COL_GA), pl.BlockSpec((1, LANE), lambda b, h, j: (0, 0))]
        args += [ofwd, proj, gain]
        out_dtype = BF16
    else:
        out_dtype = F32
    return pl.pallas_call(
        functools.partial(_hgrn_kernel, reverse=reverse, tb=tb),
        grid=(B, HEADS, nblk),
        in_specs=in_specs,
        out_specs=out_spec,
        out_shape=jax.ShapeDtypeStruct((B, S, E_HGRN), out_dtype),
        scratch_shapes=[pltpu.VMEM((HEAD_DIM, HEAD_DIM), F32), pltpu.VMEM((HEAD_DIM, HEAD_DIM), F32),
                        pltpu.VMEM((3, HGRN_CHUNK, HEAD_DIM), F32)],
        compiler_params=_cparams(("parallel", "parallel", "arbitrary")),
        name="hgrn_bwd" if reverse else "hgrn_fwd",
    )(*args)


def _mla_prep_kernel(p_ref, gq_ref, gkv_ref, wq_ref, wkv_ref, cos_ref, sin_ref,
                     qt_ref, k_ref, vt_ref, knorm_ref):
    def rms(x, gain):
        return (x * lax.rsqrt(jnp.mean(x * x, axis=-1, keepdims=True) + EPS) * gain).astype(BF16)

    blk = p_ref[...]
    cos = cos_ref[...]
    sin = sin_ref[...]
    c_q = rms(blk[:, :Q_LORA], gq_ref[...])
    c_kv = rms(blk[:, Q_LORA:Q_LORA + KV_LORA], gkv_ref[...])
    rope_k = blk[:, Q_LORA + KV_LORA:Q_LORA + KV_LORA + LANE]
    rope_k_swapped = blk[:, Q_LORA + KV_LORA + LANE:]
    k_rot = (rope_k * cos + rope_k_swapped * sin).astype(BF16)
    rot_sq = jnp.sum(k_rot.astype(F32) * k_rot.astype(F32), axis=1, keepdims=True)

    q_all = jnp.dot(c_q, wq_ref[...], preferred_element_type=F32)
    kv = jnp.dot(c_kv, wkv_ref[...], preferred_element_type=F32)
    for h in range(HEADS):
        base = 3 * LANE * h
        q_nope = q_all[:, base:base + LANE]
        q_rot = q_all[:, base + LANE:base + 2 * LANE] * cos + q_all[:, base + 2 * LANE:base + 3 * LANE] * sin
        q_h = jnp.concatenate([q_nope, q_rot], axis=1) * (ATTN_SCALE * LOG2E)
        qt_ref[h * QK_PAD:(h + 1) * QK_PAD, :] = q_h.T.astype(BF16)
        k_nope = kv[:, h * LANE:(h + 1) * LANE].astype(BF16)
        k_ref[:, h * QK_PAD:h * QK_PAD + LANE] = k_nope
        k_ref[:, h * QK_PAD + LANE:(h + 1) * QK_PAD] = k_rot
        sq = jnp.sum(k_nope.astype(F32) * k_nope.astype(F32), axis=1, keepdims=True) + rot_sq
        knorm_ref[h:h + 1, :] = jnp.broadcast_to(jnp.max(sq, axis=0, keepdims=True), (1, LANE))
        vt_ref[h, 0] = kv[:, E_MLA + h * LANE:E_MLA + (h + 1) * LANE].T.astype(BF16)


def _mla_prep(proj, gq, gkv, wq, wkv, cos, sin, tm):
    B, S, _ = proj.shape
    ns = S // tm
    full = lambda shape: pl.BlockSpec(shape, lambda b, i: (0,) * len(shape))
    return pl.pallas_call(
        _mla_prep_kernel,
        grid=(B, ns),
        in_specs=[
            pl.BlockSpec((None, tm, 1024), lambda b, i: (b, i, COL_MISC // 1024)),
            full((1, Q_LORA)), full((1, KV_LORA)),
            full((Q_LORA, HEADS * 3 * LANE)), full((KV_LORA, 2 * E_MLA)),
            pl.BlockSpec((tm, LANE), lambda b, i: (i, 0)),
            pl.BlockSpec((tm, LANE), lambda b, i: (i, 0)),
        ],
        out_specs=[
            pl.BlockSpec((None, HEADS * QK_PAD, tm), lambda b, i: (b, 0, i)),
            pl.BlockSpec((None, tm, HEADS * QK_PAD), lambda b, i: (b, i, 0)),
            pl.BlockSpec((None, HEADS, 1, HEAD_DIM, tm), lambda b, i: (b, 0, i, 0, 0)),
            pl.BlockSpec((None, None, HEADS, LANE), lambda b, i: (b, i, 0, 0)),
        ],
        out_shape=[
            jax.ShapeDtypeStruct((B, HEADS * QK_PAD, S), BF16),
            jax.ShapeDtypeStruct((B, S, HEADS * QK_PAD), BF16),
            jax.ShapeDtypeStruct((B, HEADS, ns, HEAD_DIM, tm), BF16),
            jax.ShapeDtypeStruct((B, ns, HEADS, LANE), F32),
        ],
        compiler_params=_cparams(("parallel", "parallel")),
        name="mla_prep",
    )(proj, gq, gkv, wq, wkv, cos, sin)


def _attn_kernel(qt_ref, k_ref, vt_ref, knorm_ref, gate_ref, o_ref, m_ref, l_ref, acc_ref,
                 s0_ref, s1_ref, *, tk):
    nkv = k_ref.shape[0] // tk
    tq = qt_ref.shape[1]
    groups = [slice(g * ATTN_GROUP, (g + 1) * ATTN_GROUP) for g in range(tq // ATTN_GROUP)]

    def key_tile(j):
        return k_ref[pl.ds(pl.multiple_of(j * tk, tk), tk), :]

    k_max = jnp.sqrt(jnp.max(knorm_ref[...], axis=0, keepdims=True))[:, 0:1]
    q = qt_ref[...].astype(F32)
    m_ref[...] = jnp.sqrt(jnp.sum(q * q, axis=0, keepdims=True)) * k_max
    l_ref[...] = jnp.zeros_like(l_ref)
    acc_ref[...] = jnp.zeros_like(acc_ref)

    def scores_into(j, s_ref):
        k = key_tile(j)
        for sl in groups:
            s_ref[:, sl] = jnp.dot(k, qt_ref[:, sl], preferred_element_type=F32)

    bufs = (s0_ref, s1_ref)
    n = min(ATTN_UNROLL, nkv)

    def run_tiles(j0, last):
        for u in range(n):
            cur, nxt = bufs[u % 2], bufs[(u + 1) % 2]
            vt = vt_ref[j0 + u]
            k_next = key_tile(j0 + u + 1) if (u < n - 1 or not last) else None
            for sl in groups:
                if k_next is not None:
                    nxt[:, sl] = jnp.dot(k_next, qt_ref[:, sl], preferred_element_type=F32)
                p = jnp.exp2(cur[:, sl] - m_ref[:, sl])
                l_ref[:, sl] += jnp.sum(p, axis=0, keepdims=True)
                acc_ref[:, sl] += jnp.dot(vt, p.astype(BF16), preferred_element_type=F32)

    scores_into(0, s0_ref)

    def body(jj, carry):
        run_tiles(jj * n, False)
        return carry

    lax.fori_loop(0, nkv // n - 1, body, 0)
    run_tiles(nkv - n, True)

    def finish():
        out = (acc_ref[...] / l_ref[...]).T
        gt = gate_ref[...]
        o_ref[...] = (out * gt * (1.0 / (1.0 + jnp.exp(-gt)))).astype(o_ref.dtype)

    finish()

    @pl.when(jnp.logical_not(jnp.min(l_ref[...]) >= ATTN_MIN_ROW_SUM))
    def _():
        m_ref[...] = jnp.full_like(m_ref, -jnp.inf)
        l_ref[...] = jnp.zeros_like(l_ref)
        acc_ref[...] = jnp.zeros_like(acc_ref)

        def online(j, carry):
            scores_into(j, s0_ref)
            vt = vt_ref[j]
            for sl in groups:
                st = s0_ref[:, sl]
                m_prev = m_ref[:, sl]
                m_new = jnp.maximum(m_prev, jnp.max(st, axis=0, keepdims=True))
                alpha = jnp.exp2(m_prev - m_new)
                p = jnp.exp2(st - m_new)
                l_ref[:, sl] = alpha * l_ref[:, sl] + jnp.sum(p, axis=0, keepdims=True)
                acc_ref[:, sl] = alpha * acc_ref[:, sl] + jnp.dot(vt, p.astype(BF16),
                                                                  preferred_element_type=F32)
                m_ref[:, sl] = m_new
            return carry

        lax.fori_loop(0, nkv, online, 0)
        finish()


def _attention(qt, k, vt, knorm, proj, tq, tk):
    B, S, _ = k.shape
    return pl.pallas_call(
        functools.partial(_attn_kernel, tk=tk),
        grid=(B, HEADS, S // tq),
        in_specs=[
            pl.BlockSpec((None, QK_PAD, tq), lambda b, h, i: (b, h, i)),
            pl.BlockSpec((None, S, QK_PAD), lambda b, h, i: (b, 0, h)),
            pl.BlockSpec((None, None, S // tk, HEAD_DIM, tk), lambda b, h, i: (b, h, 0, 0, 0)),
            pl.BlockSpec((None, None, S // tk, LANE), lambda b, h, i: (b, h, 0, 0)),
            pl.BlockSpec((None, tq, LANE), lambda b, h, i: (b, i, COL_GB // LANE + h)),
        ],
        out_specs=pl.BlockSpec((None, tq, LANE), lambda b, h, i: (b, i, h)),
        out_shape=jax.ShapeDtypeStruct((B, S, E_MLA), BF16),
        scratch_shapes=[pltpu.VMEM((1, tq), F32), pltpu.VMEM((1, tq), F32),
                        pltpu.VMEM((HEAD_DIM, tq), F32),
                        pltpu.VMEM((tk, tq), F32), pltpu.VMEM((tk, tq), F32)],
        compiler_params=_cparams(("parallel", "parallel", "arbitrary")),
        name="attention",
    )(qt, k, vt, knorm, proj)


def _merge_kernel(x_ref, ya_ref, yb_ref, ma_ref, mb_ref, wpa_ref, wpb_ref, wout_ref, gain_ref, o_ref,
                  *, last_layer):
    def sigmoid(t):
        return 1.0 / (1.0 + jnp.exp(-t))

    y_a = jnp.dot(ya_ref[...], wpa_ref[...], preferred_element_type=F32)
    y_b = jnp.dot(yb_ref[...], wpb_ref[...], preferred_element_type=F32)
    merged = sigmoid(ma_ref[...]) * y_a + sigmoid(mb_ref[...]) * y_b
    x = x_ref[...] + jnp.dot(merged.astype(BF16), wout_ref[...], preferred_element_type=F32)
    if last_layer:
        x = x * lax.rsqrt(jnp.mean(x * x, axis=-1, keepdims=True) + EPS) * gain_ref[...]
    o_ref[...] = x


def _merge(x, ya, yb, proj, wpa, wpb, wout, final_gain, last_layer, tm=256):
    T = x.shape[0]
    tm = min(tm, T)
    row = lambda width, colblk: pl.BlockSpec((tm, width), lambda i: (i, colblk))
    full = lambda shape: pl.BlockSpec(shape, lambda i: (0, 0))
    return pl.pallas_call(
        functools.partial(_merge_kernel, last_layer=last_layer),
        grid=(T // tm,),
        in_specs=[row(D_MODEL, 0), row(E_HGRN, 0), row(E_MLA, 0), row(D_MODEL, 0), row(D_MODEL, 1),
                  full((E_HGRN, D_MODEL)), full((E_MLA, D_MODEL)), full((D_MODEL, D_MODEL)),
                  full((1, D_MODEL))],
        out_specs=row(D_MODEL, 0),
        out_shape=jax.ShapeDtypeStruct((T, D_MODEL), F32),
        compiler_params=_cparams(("parallel",)),
        name="merge",
    )(x, ya, yb, proj, proj, wpa, wpb, wout, final_gain)


def _prep_layer_weights(w_uq, w_ukv, w_pa, w_pb, w_out):
    wq = w_uq.reshape(Q_LORA, HEADS, HEAD_DIM + QK_ROPE)
    r1 = wq[:, :, HEAD_DIM:HEAD_DIM + ROPE_HALF]
    r2 = wq[:, :, HEAD_DIM + ROPE_HALF:]
    zq = jnp.zeros((Q_LORA, HEADS, LANE - QK_ROPE), w_uq.dtype)
    wq = jnp.concatenate([wq[:, :, :HEAD_DIM], r1, r2, zq, r2, r1, zq], axis=2)
    wq = wq.reshape(Q_LORA, HEADS * 3 * LANE).astype(BF16)

    wkv = w_ukv.reshape(KV_LORA, HEADS, 2 * HEAD_DIM)
    wkv = jnp.concatenate([wkv[:, :, :HEAD_DIM].reshape(KV_LORA, E_MLA),
                           wkv[:, :, HEAD_DIM:].reshape(KV_LORA, E_MLA)], axis=1).astype(BF16)
    return wq, wkv, w_pa.astype(BF16), w_pb.astype(BF16), w_out.astype(BF16)


def _rope_tables(S):
    inv_freq = ROPE_BASE ** (-jnp.arange(ROPE_HALF, dtype=F32) / ROPE_HALF)
    ang = jnp.arange(S, dtype=F32)[:, None] * inv_freq[None, :]
    cos, sin = jnp.cos(ang), jnp.sin(ang)
    zeros = jnp.zeros((S, LANE - QK_ROPE), F32)
    return (jnp.concatenate([cos, cos, zeros], axis=1),
            jnp.concatenate([-sin, sin, zeros], axis=1))


def _trunk(x, w_proj, layers, lbs, norm_gain, hgrn_norm_gain, q_norm_gain, kv_norm_gain, final_norm_gain):
    B, S, _ = x.shape
    T = B * S
    cos, sin = _rope_tables(S)
    tk = min(512, S // 2)
    tq = min(1024, S)
    xf = x.reshape(T, D_MODEL)
    for l in range(DEPTH):
        wq, wkv, wpa, wpb, wout = layers[l]
        proj = _inproj(xf, norm_gain[l].reshape(1, D_MODEL), w_proj, l)
        proj3 = proj.reshape(B, S, PROJ_COLS)
        lb = lbs[l].reshape(2, 1, E_HGRN)
        o_fwd = _hgrn_direction(proj3, lb, reverse=False)
        ya = _hgrn_direction(proj3, lb, reverse=True, ofwd=o_fwd,
                             gain=hgrn_norm_gain[l].reshape(1, HEAD_DIM))
        qt, k, vt, knorm = _mla_prep(proj3, q_norm_gain[l].reshape(1, Q_LORA),
                              kv_norm_gain[l].reshape(1, KV_LORA), wq, wkv, cos, sin, tk)
        yb = _attention(qt, k, vt, knorm.transpose(0, 2, 1, 3), proj3, tq, tk)
        xf = _merge(xf, ya.reshape(T, E_HGRN), yb.reshape(T, E_MLA), proj, wpa, wpb, wout,
                    final_norm_gain.reshape(1, D_MODEL), last_layer=(l == DEPTH - 1))
    return xf.reshape(B, S, D_MODEL)


def kernel(x_prompt, x_sample, norm_gain, w_in, lb_logits, hgrn_norm_gain, q_norm_gain, w_uq,
           kv_norm_gain, w_ukv, w_pa, w_pb, w_out, final_norm_gain):
    lbs = _lower_bounds(lb_logits)
    layers = [_prep_layer_weights(w_uq[l], w_ukv[l], w_pa[l], w_pb[l], w_out[l]) for l in range(DEPTH)]
    args = (_pack_w_in(w_in), layers, lbs, norm_gain, hgrn_norm_gain, q_norm_gain, kv_norm_gain,
            final_norm_gain)
    return (_trunk(x_prompt, *args), _trunk(x_sample, *args))
```

```python
import functools

import jax
import jax.numpy as jnp
import numpy as np
from jax import lax
from jax.experimental import pallas as pl
from jax.experimental.pallas import tpu as pltpu

F32 = jnp.float32
BF16 = jnp.bfloat16

D_MODEL = 2048
DEPTH = 4
E_HGRN = 1024
HEADS = 8
HEAD_DIM = 128
QK_ROPE = 64
ROPE_HALF = QK_ROPE // 2
Q_LORA = 512
KV_LORA = 256
E_MLA = 1024
ROPE_BASE = 10000.0
EPS = 1e-6
ATTN_SCALE = (HEAD_DIM + QK_ROPE) ** -0.5

PROJ_COLS = 11264
COL_Q, COL_ZF, COL_ZB, COL_I, COL_GA = 4096, 5120, 6144, 7168, 8192
COL_MISC, COL_GB = 9216, 10240
LANE = 128
QK_PAD = 2 * HEAD_DIM

HGRN_CHUNK = 128
HGRN_SUB = 32
HGRN_UNROLL = 16
HGRN_MAX_SUB_DECAY = 120.0
ATTN_GROUP = 256
ATTN_UNROLL = 4
ATTN_MIN_ROW_SUM = 2.0 ** -80
LOG2E = 1.4426950408889634
VMEM_LIMIT = 56 * 1024 * 1024


def _cparams(sem):
    return pltpu.CompilerParams(dimension_semantics=sem, vmem_limit_bytes=VMEM_LIMIT)


def _lower_bound_kernel(x_ref, o_ref):
    x = x_ref[...]
    e = jnp.exp(x - jnp.max(x, axis=0, keepdims=True))
    p = e / jnp.sum(e, axis=0, keepdims=True)
    c = p[0:1]
    rows = [c]
    for l in range(1, DEPTH):
        c = c + p[l:l + 1]
        rows.append(c)
    o_ref[...] = jnp.concatenate([r - rows[0] for r in rows], axis=0)


def _lower_bounds(lb_logits):
    x = lb_logits.reshape(DEPTH, 2 * E_HGRN)
    return pl.pallas_call(
        _lower_bound_kernel,
        out_shape=jax.ShapeDtypeStruct((DEPTH, 2 * E_HGRN), F32),
        name="lower_bounds",
    )(x)


IN_OFFSETS = tuple(int(v) for v in np.cumsum(
    (0, E_HGRN, E_HGRN, E_HGRN, E_HGRN, E_HGRN, Q_LORA, KV_LORA, QK_ROPE, E_MLA, D_MODEL, D_MODEL)))
IN_COLS = IN_OFFSETS[-1]


def _pack_w_in_kernel(w_ref, o_ref):
    o = IN_OFFSETS

    def put(dst, first, last):
        o_ref[:, dst:dst + o[last] - o[first]] = w_ref[:, o[first]:o[last]].astype(BF16)

    put(0, 9, 11)
    put(COL_Q, 0, 5)
    put(COL_MISC, 5, 7)
    rope = w_ref[:, o[7]:o[8]]
    zeros = jnp.zeros((rope.shape[0], LANE - QK_ROPE), rope.dtype)
    rope_at = COL_MISC + Q_LORA + KV_LORA
    o_ref[:, rope_at:rope_at + LANE] = jnp.concatenate([rope, zeros], axis=1).astype(BF16)
    o_ref[:, rope_at + LANE:rope_at + 2 * LANE] = jnp.concatenate(
        [rope[:, ROPE_HALF:], rope[:, :ROPE_HALF], zeros], axis=1).astype(BF16)
    put(COL_GB, 8, 9)


def _pack_w_in(w_in, tr=256):
    return pl.pallas_call(
        _pack_w_in_kernel,
        grid=(DEPTH, D_MODEL // tr),
        in_specs=[pl.BlockSpec((None, tr, IN_COLS), lambda l, i: (l, i, 0))],
        out_specs=pl.BlockSpec((None, tr, PROJ_COLS), lambda l, i: (l, i, 0)),
        out_shape=jax.ShapeDtypeStruct((DEPTH, D_MODEL, PROJ_COLS), BF16),
        compiler_params=_cparams(("parallel", "parallel")),
        name="pack_w_in",
    )(w_in)


def _inproj_kernel(x_ref, g_ref, w_ref, o_ref, h_ref):
    @pl.when(pl.program_id(1) == 0)
    def _():
        x = x_ref[...]
        ms = jnp.mean(x * x, axis=-1, keepdims=True)
        h_ref[...] = (x * lax.rsqrt(ms + EPS) * g_ref[...]).astype(BF16)

    o_ref[...] = jnp.dot(h_ref[...], w_ref[...], preferred_element_type=F32)


def _inproj(x, gain, w, layer, tm=1024, tn=1024):
    T = x.shape[0]
    tm = min(tm, T)
    return pl.pallas_call(
        _inproj_kernel,
        grid=(T // tm, PROJ_COLS // tn),
        in_specs=[
            pl.BlockSpec((tm, D_MODEL), lambda i, j: (i, 0)),
            pl.BlockSpec((1, D_MODEL), lambda i, j: (0, 0)),
            pl.BlockSpec((None, D_MODEL, tn), lambda i, j: (layer, 0, j)),
        ],
        out_specs=pl.BlockSpec((tm, tn), lambda i, j: (i, j)),
        out_shape=jax.ShapeDtypeStruct((T, PROJ_COLS), F32),
        scratch_shapes=[pltpu.VMEM((tm, D_MODEL), BF16)],
        compiler_params=_cparams(("parallel", "arbitrary")),
        name="inproj",
    )(x, gain, w)


def _hgrn_kernel(*refs, reverse, tb):
    if reverse:
        q_ref, z_ref, v_ref, lb_ref, ofwd_ref, gate_ref, gain_ref, o_ref, state_ref, prev_ref, pair_ref = refs
    else:
        q_ref, z_ref, v_ref, lb_ref, o_ref, state_ref, prev_ref, pair_ref = refs
    C, c = HGRN_CHUNK, HGRN_SUB
    nsub = C // c
    nch = tb // C
    U = min(HGRN_UNROLL, nch)

    @pl.when(pl.program_id(2) == 0)
    def _():
        state_ref[...] = jnp.zeros_like(state_ref)

    lb = lb_ref[...]
    log_lb = jnp.log(lb) * LOG2E
    log_1m_lb = jnp.log1p(-lb) * LOG2E
    one_m_lb = 1.0 - lb

    row = lax.broadcasted_iota(jnp.int32, (C, C), 0)
    col = lax.broadcasted_iota(jnp.int32, (C, C), 1)
    tri = (col >= row) if reverse else (col <= row)
    tri_b = tri.astype(BF16)

    def gates(off):
        z = z_ref[pl.ds(off, C), :]
        q_raw = q_ref[pl.ds(off, C), :]
        v = v_ref[pl.ds(off, C), :].astype(BF16)

        e = jnp.exp2(jnp.abs(z) * -LOG2E)
        r = 1.0 / (1.0 + e)
        log_sig = jnp.minimum(z, 0.0) * LOG2E - jnp.log2(1.0 + e)
        k = one_m_lb * jnp.where(z >= 0.0, e * r, r)
        t = log_1m_lb + log_sig
        g = jnp.maximum(log_lb, t) + jnp.log2(1.0 + jnp.exp2(-jnp.abs(log_lb - t)))
        q = q_raw * (1.0 / (1.0 + jnp.exp(-q_raw)))

        g_hi = g.astype(BF16)
        rem = g - g_hi.astype(F32)
        g_mid = rem.astype(BF16)
        g_lo = (rem - g_mid.astype(F32)).astype(BF16)
        b3 = jnp.dot(tri_b, jnp.concatenate([g_hi, g_mid, g_lo], axis=1), preferred_element_type=F32)
        return q, k, v, b3

    def sub_chunk_scores(q, k, v, b3, exact):
        b = b3[:, :LANE] + b3[:, LANE:2 * LANE] + b3[:, 2 * LANE:]
        b_last = b[0:1] if reverse else b[C - 1:C]
        rows = []
        worst = jnp.zeros_like(b_last)
        k_bf = k.astype(BF16)
        for i in range(0 if exact else nsub):
            if reverse:
                ref_i = b[(i + 1) * c:(i + 1) * c + 1] if i < nsub - 1 else jnp.zeros_like(b_last)
                end_i = b[i * c:i * c + 1]
            else:
                ref_i = b[i * c - 1:i * c] if i > 0 else jnp.zeros_like(b_last)
                end_i = b[(i + 1) * c - 1:(i + 1) * c]
            worst = jnp.minimum(worst, end_i - ref_i)
            blk = slice(i * c, (i + 1) * c)
            q_i = (q[blk] * jnp.exp2(b[blk] - ref_i)).astype(BF16)
            if reverse:
                k_i = (k[i * c:] * jnp.exp2(ref_i - b[i * c:])).astype(BF16)
                k_i = jnp.concatenate([k_bf[:i * c], k_i], axis=0) if i > 0 else k_i
            else:
                k_i = (k[:(i + 1) * c] * jnp.exp2(ref_i - b[:(i + 1) * c])).astype(BF16)
                k_i = jnp.concatenate([k_i, k_bf[(i + 1) * c:]], axis=0) if i < nsub - 1 else k_i
            rows.append(lax.dot_general(q_i, k_i, (((1,), (1,)), ((), ())), preferred_element_type=F32))
        q_in = (q * jnp.exp2(b)).astype(BF16)
        k_end = (k * jnp.exp2(b_last - b)).astype(BF16)
        return (rows, v), (q, k, b, v), worst, (q_in, k_end, jnp.exp2(b_last), v)

    def intra(rows, v):
        scores = jnp.where(tri, jnp.concatenate(rows, axis=0), 0.0).astype(BF16)
        return jnp.dot(scores, v, preferred_element_type=F32)

    def intra_exact(q, k, b, v):
        pair_ref[0] = b
        pair_ref[1] = k
        pair_ref[2] = v.astype(F32)
        t_idx = lax.broadcasted_iota(jnp.int32, (C, 1), 0)

        def source(s, acc):
            b_s = pair_ref[0, pl.ds(s, 1), :]
            w = q * pair_ref[1, pl.ds(s, 1), :] * jnp.exp2(jnp.minimum(b - b_s, 0.0))
            score = jnp.sum(w, axis=1, keepdims=True)
            seen = (t_idx <= s) if reverse else (t_idx >= s)
            return acc + jnp.where(seen, score, 0.0) * pair_ref[2, pl.ds(s, 1), :]

        return lax.fori_loop(0, C, source, jnp.zeros((C, HEAD_DIM), F32))

    def run_chunks(ci, exact):
        offs = []
        for u in range(U):
            cu = ci * U + u
            offs.append(pl.multiple_of(((nch - 1 - cu) if reverse else cu) * C, C))
        parts = [gates(off) for off in offs]
        parts = [sub_chunk_scores(*p, exact) for p in parts]
        worst = parts[0][2]
        for p in parts[1:]:
            worst = jnp.minimum(worst, p[2])
        o_intras = [intra_exact(*p[1]) if exact else intra(*p[0]) for p in parts]
        adds = [lax.dot_general(v, k_end, (((0,), (0,)), ((), ())), preferred_element_type=F32)
                for _, k_end, _, v in (p[3] for p in parts)]
        state = state_ref[...]
        for off, o_intra, add, (q_in, _, decay, _) in zip(offs, o_intras, adds, (p[3] for p in parts)):
            o = o_intra + lax.dot_general(q_in, state.astype(BF16), (((1,), (1,)), ((), ())),
                                          preferred_element_type=F32)
            state = state * decay + add
            if reverse:
                tot = o + ofwd_ref[pl.ds(off, C), :]
                y = tot * lax.rsqrt(jnp.mean(tot * tot, axis=-1, keepdims=True) + EPS) * gain_ref[...]
                gt = gate_ref[pl.ds(off, C), :]
                o_ref[pl.ds(off, C), :] = (y * gt * (1.0 / (1.0 + jnp.exp(-gt)))).astype(o_ref.dtype)
            else:
                o_ref[pl.ds(off, C), :] = o
        state_ref[...] = state
        return jnp.min(worst)

    def chunks(ci, carry):
        prev_ref[...] = state_ref[...]
        lowest = run_chunks(ci, exact=False)

        @pl.when(jnp.logical_not(lowest >= -HGRN_MAX_SUB_DECAY))
        def _():
            state_ref[...] = prev_ref[...]
            run_chunks(ci, exact=True)

        return carry

    lax.fori_loop(0, nch // U, chunks, 0)


def _hgrn_direction(proj, lb, reverse, ofwd=None, gain=None, tb=2048):
    B, S, _ = proj.shape
    tb = min(tb, S)
    nblk = S // tb
    blk = (lambda j: nblk - 1 - j) if reverse else (lambda j: j)

    def col_spec(col0):
        return pl.BlockSpec((None, tb, LANE), lambda b, h, j: (b, blk(j), col0 // LANE + h))

    z_col = COL_ZB if reverse else COL_ZF
    lb_row = 1 if reverse else 0
    in_specs = [col_spec(COL_Q), col_spec(z_col), col_spec(COL_I),
                pl.BlockSpec((None, 1, LANE), lambda b, h, j: (lb_row, 0, h))]
    args = [proj, proj, proj, lb]
    out_spec = pl.BlockSpec((None, tb, LANE), lambda b, h, j: (b, blk(j), h))
    if reverse:
        in_specs += [col_spec(0), col_spec(COL_GA), pl.BlockSpec((1, LANE), lambda b, h, j: (0, 0))]
        args += [ofwd, proj, gain]
        out_dtype = BF16
    else:
        out_dtype = F32
    return pl.pallas_call(
        functools.partial(_hgrn_kernel, reverse=reverse, tb=tb),
        grid=(B, HEADS, nblk),
        in_specs=in_specs,
        out_specs=out_spec,
        out_shape=jax.ShapeDtypeStruct((B, S, E_HGRN), out_dtype),
        scratch_shapes=[pltpu.VMEM((HEAD_DIM, HEAD_DIM), F32), pltpu.VMEM((HEAD_DIM, HEAD_DIM), F32),
                        pltpu.VMEM((3, HGRN_CHUNK, HEAD_DIM), F32)],
        compiler_params=_cparams(("parallel", "parallel", "arbitrary")),
        name="hgrn_bwd" if reverse else "hgrn_fwd",
    )(*args)


def _mla_prep_kernel(p_ref, gq_ref, gkv_ref, wq_ref, wkv_ref, cos_ref, sin_ref,
                     qt_ref, k_ref, vt_ref, knorm_ref):
    def rms(x, gain):
        return (x * lax.rsqrt(jnp.mean(x * x, axis=-1, keepdims=True) + EPS) * gain).astype(BF16)

    blk = p_ref[...]
    cos = cos_ref[...]
    sin = sin_ref[...]
    c_q = rms(blk[:, :Q_LORA], gq_ref[...])
    c_kv = rms(blk[:, Q_LORA:Q_LORA + KV_LORA], gkv_ref[...])
    rope_k = blk[:, Q_LORA + KV_LORA:Q_LORA + KV_LORA + LANE]
    rope_k_swapped = blk[:, Q_LORA + KV_LORA + LANE:]
    k_rot = (rope_k * cos + rope_k_swapped * sin).astype(BF16)
    rot_sq = jnp.sum(k_rot.astype(F32) * k_rot.astype(F32), axis=1, keepdims=True)

    q_all = jnp.dot(c_q, wq_ref[...], preferred_element_type=F32)
    kv = jnp.dot(c_kv, wkv_ref[...], preferred_element_type=F32)
    for h in range(HEADS):
        base = 3 * LANE * h
        q_nope = q_all[:, base:base + LANE]
        q_rot = q_all[:, base + LANE:base + 2 * LANE] * cos + q_all[:, base + 2 * LANE:base + 3 * LANE] * sin
        q_h = jnp.concatenate([q_nope, q_rot], axis=1) * (ATTN_SCALE * LOG2E)
        qt_ref[h * QK_PAD:(h + 1) * QK_PAD, :] = q_h.T.astype(BF16)
        k_nope = kv[:, h * LANE:(h + 1) * LANE].astype(BF16)
        k_ref[:, h * QK_PAD:h * QK_PAD + LANE] = k_nope
        k_ref[:, h * QK_PAD + LANE:(h + 1) * QK_PAD] = k_rot
        sq = jnp.sum(k_nope.astype(F32) * k_nope.astype(F32), axis=1, keepdims=True) + rot_sq
        knorm_ref[h:h + 1, :] = jnp.broadcast_to(jnp.max(sq, axis=0, keepdims=True), (1, LANE))
        vt_ref[h, 0] = kv[:, E_MLA + h * LANE:E_MLA + (h + 1) * LANE].T.astype(BF16)


def _mla_prep(proj, gq, gkv, wq, wkv, cos, sin, tm):
    B, S, _ = proj.shape
    ns = S // tm
    full = lambda shape: pl.BlockSpec(shape, lambda b, i: (0,) * len(shape))
    return pl.pallas_call(
        _mla_prep_kernel,
        grid=(B, ns),
        in_specs=[
            pl.BlockSpec((None, tm, 1024), lambda b, i: (b, i, COL_MISC // 1024)),
            full((1, Q_LORA)), full((1, KV_LORA)),
            full((Q_LORA, HEADS * 3 * LANE)), full((KV_LORA, 2 * E_MLA)),
            pl.BlockSpec((tm, LANE), lambda b, i: (i, 0)),
            pl.BlockSpec((tm, LANE), lambda b, i: (i, 0)),
        ],
        out_specs=[
            pl.BlockSpec((None, HEADS * QK_PAD, tm), lambda b, i: (b, 0, i)),
            pl.BlockSpec((None, tm, HEADS * QK_PAD), lambda b, i: (b, i, 0)),
            pl.BlockSpec((None, HEADS, 1, HEAD_DIM, tm), lambda b, i: (b, 0, i, 0, 0)),
            pl.BlockSpec((None, None, HEADS, LANE), lambda b, i: (b, i, 0, 0)),
        ],
        out_shape=[
            jax.ShapeDtypeStruct((B, HEADS * QK_PAD, S), BF16),
            jax.ShapeDtypeStruct((B, S, HEADS * QK_PAD), BF16),
            jax.ShapeDtypeStruct((B, HEADS, ns, HEAD_DIM, tm), BF16),
            jax.ShapeDtypeStruct((B, ns, HEADS, LANE), F32),
        ],
        compiler_params=_cparams(("parallel", "parallel")),
        name="mla_prep",
    )(proj, gq, gkv, wq, wkv, cos, sin)


def _attn_kernel(qt_ref, k_ref, vt_ref, knorm_ref, gate_ref, o_ref, m_ref, l_ref, acc_ref,
                 s0_ref, s1_ref, *, tk):
    nkv = k_ref.shape[0] // tk
    tq = qt_ref.shape[1]
    groups = [slice(g * ATTN_GROUP, (g + 1) * ATTN_GROUP) for g in range(tq // ATTN_GROUP)]

    def key_tile(j):
        return k_ref[pl.ds(pl.multiple_of(j * tk, tk), tk), :]

    k_max = jnp.sqrt(jnp.max(knorm_ref[...], axis=0, keepdims=True))[:, 0:1]
    q = qt_ref[...].astype(F32)
    m_ref[...] = jnp.sqrt(jnp.sum(q * q, axis=0, keepdims=True)) * k_max
    l_ref[...] = jnp.zeros_like(l_ref)
    acc_ref[...] = jnp.zeros_like(acc_ref)

    def scores_into(j, s_ref):
        k = key_tile(j)
        for sl in groups:
            s_ref[:, sl] = jnp.dot(k, qt_ref[:, sl], preferred_element_type=F32)

    bufs = (s0_ref, s1_ref)
    n = min(ATTN_UNROLL, nkv)

    def run_tiles(j0, last):
        for u in range(n):
            cur, nxt = bufs[u % 2], bufs[(u + 1) % 2]
            vt = vt_ref[j0 + u]
            k_next = key_tile(j0 + u + 1) if (u < n - 1 or not last) else None
            for sl in groups:
                if k_next is not None:
                    nxt[:, sl] = jnp.dot(k_next, qt_ref[:, sl], preferred_element_type=F32)
                p = jnp.exp2(cur[:, sl] - m_ref[:, sl])
                l_ref[:, sl] += jnp.sum(p, axis=0, keepdims=True)
                acc_ref[:, sl] += jnp.dot(vt, p.astype(BF16), preferred_element_type=F32)

    scores_into(0, s0_ref)

    def body(jj, carry):
        run_tiles(jj * n, False)
        return carry

    lax.fori_loop(0, nkv // n - 1, body, 0)
    run_tiles(nkv - n, True)

    def finish():
        out = (acc_ref[...] / l_ref[...]).T
        gt = gate_ref[...]
        o_ref[...] = (out * gt * (1.0 / (1.0 + jnp.exp(-gt)))).astype(o_ref.dtype)

    finish()

    @pl.when(jnp.logical_not(jnp.min(l_ref[...]) >= ATTN_MIN_ROW_SUM))
    def _():
        m_ref[...] = jnp.full_like(m_ref, -jnp.inf)
        l_ref[...] = jnp.zeros_like(l_ref)
        acc_ref[...] = jnp.zeros_like(acc_ref)

        def online(j, carry):
            scores_into(j, s0_ref)
            vt = vt_ref[j]
            for sl in groups:
                st = s0_ref[:, sl]
                m_prev = m_ref[:, sl]
                m_new = jnp.maximum(m_prev, jnp.max(st, axis=0, keepdims=True))
                alpha = jnp.exp2(m_prev - m_new)
                p = jnp.exp2(st - m_new)
                l_ref[:, sl] = alpha * l_ref[:, sl] + jnp.sum(p, axis=0, keepdims=True)
                acc_ref[:, sl] = alpha * acc_ref[:, sl] + jnp.dot(vt, p.astype(BF16),
                                                                  preferred_element_type=F32)
                m_ref[:, sl] = m_new
            return carry

        lax.fori_loop(0, nkv, online, 0)
        finish()


def _attention(qt, k, vt, knorm, proj, tq, tk):
    B, S, _ = k.shape
    return pl.pallas_call(
        functools.partial(_attn_kernel, tk=tk),
        grid=(B, HEADS, S // tq),
        in_specs=[
            pl.BlockSpec((None, QK_PAD, tq), lambda b, h, i: (b, h, i)),
            pl.BlockSpec((None, S, QK_PAD), lambda b, h, i: (b, 0, h)),
            pl.BlockSpec((None, None, S // tk, HEAD_DIM, tk), lambda b, h, i: (b, h, 0, 0, 0)),
            pl.BlockSpec((None, None, S // tk, LANE), lambda b, h, i: (b, h, 0, 0)),
            pl.BlockSpec((None, tq, LANE), lambda b, h, i: (b, i, COL_GB // LANE + h)),
        ],
        out_specs=pl.BlockSpec((None, tq, LANE), lambda b, h, i: (b, i, h)),
        out_shape=jax.ShapeDtypeStruct((B, S, E_MLA), BF16),
        scratch_shapes=[pltpu.VMEM((1, tq), F32), pltpu.VMEM((1, tq), F32),
                        pltpu.VMEM((HEAD_DIM, tq), F32),
                        pltpu.VMEM((tk, tq), F32), pltpu.VMEM((tk, tq), F32)],
        compiler_params=_cparams(("parallel", "parallel", "arbitrary")),
        name="attention",
    )(qt, k, vt, knorm, proj)


def _merge_kernel(x_ref, ya_ref, yb_ref, ma_ref, mb_ref, wpa_ref, wpb_ref, wout_ref, gain_ref, o_ref,
                  *, last_layer):
    def sigmoid(t):
        return 1.0 / (1.0 + jnp.exp(-t))

    y_a = jnp.dot(ya_ref[...], wpa_ref[...], preferred_element_type=F32)
    y_b = jnp.dot(yb_ref[...], wpb_ref[...], preferred_element_type=F32)
    merged = sigmoid(ma_ref[...]) * y_a + sigmoid(mb_ref[...]) * y_b
    x = x_ref[...] + jnp.dot(merged.astype(BF16), wout_ref[...], preferred_element_type=F32)
    if last_layer:
        x = x * lax.rsqrt(jnp.mean(x * x, axis=-1, keepdims=True) + EPS) * gain_ref[...]
    o_ref[...] = x


def _merge(x, ya, yb, proj, wpa, wpb, wout, final_gain, last_layer, tm=256):
    T = x.shape[0]
    tm = min(tm, T)
    row = lambda width, colblk: pl.BlockSpec((tm, width), lambda i: (i, colblk))
    full = lambda shape: pl.BlockSpec(shape, lambda i: (0, 0))
    return pl.pallas_call(
        functools.partial(_merge_kernel, last_layer=last_layer),
        grid=(T // tm,),
        in_specs=[row(D_MODEL, 0), row(E_HGRN, 0), row(E_MLA, 0), row(D_MODEL, 0), row(D_MODEL, 1),
                  full((E_HGRN, D_MODEL)), full((E_MLA, D_MODEL)), full((D_MODEL, D_MODEL)),
                  full((1, D_MODEL))],
        out_specs=row(D_MODEL, 0),
        out_shape=jax.ShapeDtypeStruct((T, D_MODEL), F32),
        compiler_params=_cparams(("parallel",)),
        name="merge",
    )(x, ya, yb, proj, proj, wpa, wpb, wout, final_gain)


def _prep_layer_weights(w_uq, w_ukv, w_pa, w_pb, w_out):
    wq = w_uq.reshape(Q_LORA, HEADS, HEAD_DIM + QK_ROPE)
    r1 = wq[:, :, HEAD_DIM:HEAD_DIM + ROPE_HALF]
    r2 = wq[:, :, HEAD_DIM + ROPE_HALF:]
    zq = jnp.zeros((Q_LORA, HEADS, LANE - QK_ROPE), w_uq.dtype)
    wq = jnp.concatenate([wq[:, :, :HEAD_DIM], r1, r2, zq, r2, r1, zq], axis=2)
    wq = wq.reshape(Q_LORA, HEADS * 3 * LANE).astype(BF16)

    wkv = w_ukv.reshape(KV_LORA, HEADS, 2 * HEAD_DIM)
    wkv = jnp.concatenate([wkv[:, :, :HEAD_DIM].reshape(KV_LORA, E_MLA),
                           wkv[:, :, HEAD_DIM:].reshape(KV_LORA, E_MLA)], axis=1).astype(BF16)
    return wq, wkv, w_pa.astype(BF16), w_pb.astype(BF16), w_out.astype(BF16)


def _rope_tables(S):
    inv_freq = ROPE_BASE ** (-jnp.arange(ROPE_HALF, dtype=F32) / ROPE_HALF)
    ang = jnp.arange(S, dtype=F32)[:, None] * inv_freq[None, :]
    cos, sin = jnp.cos(ang), jnp.sin(ang)
    zeros = jnp.zeros((S, LANE - QK_ROPE), F32)
    return (jnp.concatenate([cos, cos, zeros], axis=1),
            jnp.concatenate([-sin, sin, zeros], axis=1))


def _trunk(x, w_proj, layers, lbs, norm_gain, hgrn_norm_gain, q_norm_gain, kv_norm_gain, final_norm_gain):
    B, S, _ = x.shape
    T = B * S
    cos, sin = _rope_tables(S)
    tk = min(512, S // 2)
    tq = min(1024, S)
    xf = x.reshape(T, D_MODEL)
    for l in range(DEPTH):
        wq, wkv, wpa, wpb, wout = layers[l]
        proj = _inproj(xf, norm_gain[l].reshape(1, D_MODEL), w_proj, l)
        proj3 = proj.reshape(B, S, PROJ_COLS)
        lb = lbs[l].reshape(2, 1, E_HGRN)
        o_fwd = _hgrn_direction(proj3, lb, reverse=False)
        ya = _hgrn_direction(proj3, lb, reverse=True, ofwd=o_fwd,
                             gain=hgrn_norm_gain[l].reshape(1, HEAD_DIM))
        qt, k, vt, knorm = _mla_prep(proj3, q_norm_gain[l].reshape(1, Q_LORA),
                              kv_norm_gain[l].reshape(1, KV_LORA), wq, wkv, cos, sin, tk)
        yb = _attention(qt, k, vt, knorm.transpose(0, 2, 1, 3), proj3, tq, tk)
        xf = _merge(xf, ya.reshape(T, E_HGRN), yb.reshape(T, E_MLA), proj, wpa, wpb, wout,
                    final_norm_gain.reshape(1, D_MODEL), last_layer=(l == DEPTH - 1))
    return xf.reshape(B, S, D_MODEL)


def kernel(x_prompt, x_sample, norm_gain, w_in, lb_logits, hgrn_norm_gain, q_norm_gain, w_uq,
           kv_norm_gain, w_ukv, w_pa, w_pb, w_out, final_norm_gain):
    lbs = _lower_bounds(lb_logits)
    layers = [_prep_layer_weights(w_uq[l], w_ukv[l], w_pa[l], w_pb[l], w_out[l]) for l in range(DEPTH)]
    args = (_pack_w_in(w_in.astype(BF16)), layers, lbs, norm_gain, hgrn_norm_gain, q_norm_gain, kv_norm_gain,
            final_norm_gain)
    return (_trunk(x_prompt, *args), _trunk(x_sample, *args))
```

```python
import functools

import jax
import jax.numpy as jnp
import numpy as np
from jax import lax
from jax.experimental import pallas as pl
from jax.experimental.pallas import tpu as pltpu

F32 = jnp.float32
BF16 = jnp.bfloat16

D_MODEL = 2048
DEPTH = 4
E_HGRN = 1024
HEADS = 8
HEAD_DIM = 128
QK_ROPE = 64
ROPE_HALF = QK_ROPE // 2
Q_LORA = 512
KV_LORA = 256
E_MLA = 1024
ROPE_BASE = 10000.0
EPS = 1e-6
ATTN_SCALE = (HEAD_DIM + QK_ROPE) ** -0.5

PROJ_COLS = 11264
COL_Q, COL_ZF, COL_ZB, COL_I, COL_GA = 4096, 5120, 6144, 7168, 8192
COL_MISC, COL_GB = 9216, 10240
LANE = 128
QK_PAD = 2 * HEAD_DIM

HGRN_CHUNK = 128
HGRN_SUB = 32
HGRN_UNROLL = 16
HGRN_MAX_SUB_DECAY = 120.0
ATTN_GROUP = 256
ATTN_UNROLL = 4
ATTN_MIN_ROW_SUM = 2.0 ** -80
LOG2E = 1.4426950408889634

PACK_ROWS = 256
INPROJ_ROWS = 1024
INPROJ_COL_TILE = 1024
HGRN_BLOCK = 2048
ATTN_QUERY_TILE = 1024
ATTN_KEY_TILE = 512
MERGE_ROWS = 256
VMEM_V7X = 64 * 1024 * 1024
VMEM_LIMIT = VMEM_V7X - 8 * 1024 * 1024


def _cparams(sem):
    return pltpu.CompilerParams(dimension_semantics=sem, vmem_limit_bytes=VMEM_LIMIT)


def _lower_bound_kernel(x_ref, o_ref):
    x = x_ref[...]
    e = jnp.exp(x - jnp.max(x, axis=0, keepdims=True))
    p = e / jnp.sum(e, axis=0, keepdims=True)
    c = p[0:1]
    rows = [c]
    for l in range(1, DEPTH):
        c = c + p[l:l + 1]
        rows.append(c)
    o_ref[...] = jnp.concatenate([r - rows[0] for r in rows], axis=0)


def _lower_bounds(lb_logits):
    x = lb_logits.reshape(DEPTH, 2 * E_HGRN)
    return pl.pallas_call(
        _lower_bound_kernel,
        out_shape=jax.ShapeDtypeStruct((DEPTH, 2 * E_HGRN), F32),
        name="lower_bounds",
    )(x)


IN_OFFSETS = tuple(int(v) for v in np.cumsum(
    (0, E_HGRN, E_HGRN, E_HGRN, E_HGRN, E_HGRN, Q_LORA, KV_LORA, QK_ROPE, E_MLA, D_MODEL, D_MODEL)))
IN_COLS = IN_OFFSETS[-1]


def _pack_w_in_kernel(w_ref, o_ref):
    o = IN_OFFSETS

    def put(dst, first, last):
        o_ref[:, dst:dst + o[last] - o[first]] = w_ref[:, o[first]:o[last]].astype(BF16)

    put(0, 9, 11)
    put(COL_Q, 0, 5)
    put(COL_MISC, 5, 7)
    rope = w_ref[:, o[7]:o[8]]
    zeros = jnp.zeros((rope.shape[0], LANE - QK_ROPE), F32)
    rope_at = COL_MISC + Q_LORA + KV_LORA
    o_ref[:, rope_at:rope_at + LANE] = jnp.concatenate([rope, zeros], axis=1).astype(BF16)
    o_ref[:, rope_at + LANE:rope_at + 2 * LANE] = jnp.concatenate(
        [rope[:, ROPE_HALF:], rope[:, :ROPE_HALF], zeros], axis=1).astype(BF16)
    put(COL_GB, 8, 9)


def _pack_w_in(w_in):
    tr = PACK_ROWS
    return pl.pallas_call(
        _pack_w_in_kernel,
        grid=(DEPTH, D_MODEL // tr),
        in_specs=[pl.BlockSpec((None, tr, IN_COLS), lambda l, i: (l, i, 0))],
        out_specs=pl.BlockSpec((None, tr, PROJ_COLS), lambda l, i: (l, i, 0)),
        out_shape=jax.ShapeDtypeStruct((DEPTH, D_MODEL, PROJ_COLS), BF16),
        compiler_params=_cparams(("parallel", "parallel")),
        name="pack_w_in",
    )(w_in)


def _inproj_kernel(x_ref, g_ref, w_ref, o_ref, h_ref):
    @pl.when(pl.program_id(1) == 0)
    def _():
        x = x_ref[...]
        ms = jnp.mean(x * x, axis=-1, keepdims=True)
        h_ref[...] = (x * lax.rsqrt(ms + EPS) * g_ref[...]).astype(BF16)

    o_ref[...] = jnp.dot(h_ref[...], w_ref[...], preferred_element_type=F32)


def _inproj(x, gain, w, layer):
    T = x.shape[0]
    tm, tn = min(INPROJ_ROWS, T), INPROJ_COL_TILE
    return pl.pallas_call(
        _inproj_kernel,
        grid=(T // tm, PROJ_COLS // tn),
        in_specs=[
            pl.BlockSpec((tm, D_MODEL), lambda i, j: (i, 0)),
            pl.BlockSpec((1, D_MODEL), lambda i, j: (0, 0)),
            pl.BlockSpec((None, D_MODEL, tn), lambda i, j: (layer, 0, j)),
        ],
        out_specs=pl.BlockSpec((tm, tn), lambda i, j: (i, j)),
        out_shape=jax.ShapeDtypeStruct((T, PROJ_COLS), F32),
        scratch_shapes=[pltpu.VMEM((tm, D_MODEL), BF16)],
        compiler_params=_cparams(("parallel", "arbitrary")),
        name="inproj",
    )(x, gain, w)


def _hgrn_kernel(*refs, reverse, tb):
    if reverse:
        q_ref, z_ref, v_ref, lb_ref, ofwd_ref, gate_ref, gain_ref, o_ref, state_ref, prev_ref, pair_ref = refs
    else:
        q_ref, z_ref, v_ref, lb_ref, o_ref, state_ref, prev_ref, pair_ref = refs
    C, c = HGRN_CHUNK, HGRN_SUB
    nsub = C // c
    nch = tb // C
    U = min(HGRN_UNROLL, nch)

    @pl.when(pl.program_id(2) == 0)
    def _():
        state_ref[...] = jnp.zeros_like(state_ref)

    lb = lb_ref[...]
    log_lb = jnp.log(lb) * LOG2E
    log_1m_lb = jnp.log1p(-lb) * LOG2E
    one_m_lb = 1.0 - lb

    row = lax.broadcasted_iota(jnp.int32, (C, C), 0)
    col = lax.broadcasted_iota(jnp.int32, (C, C), 1)
    tri = (col >= row) if reverse else (col <= row)
    tri_b = tri.astype(BF16)

    def gates(off):
        z = z_ref[pl.ds(off, C), :]
        q_raw = q_ref[pl.ds(off, C), :]
        v = v_ref[pl.ds(off, C), :].astype(BF16)

        e = jnp.exp2(jnp.abs(z) * -LOG2E)
        r = 1.0 / (1.0 + e)
        log_sig = jnp.minimum(z, 0.0) * LOG2E - jnp.log2(1.0 + e)
        k = one_m_lb * jnp.where(z >= 0.0, e * r, r)
        t = log_1m_lb + log_sig
        g = jnp.maximum(log_lb, t) + jnp.log2(1.0 + jnp.exp2(-jnp.abs(log_lb - t)))
        q = q_raw * (1.0 / (1.0 + jnp.exp(-q_raw)))

        g_hi = g.astype(BF16)
        rem = g - g_hi.astype(F32)
        g_mid = rem.astype(BF16)
        g_lo = (rem - g_mid.astype(F32)).astype(BF16)
        b3 = jnp.dot(tri_b, jnp.concatenate([g_hi, g_mid, g_lo], axis=1), preferred_element_type=F32)
        return q, k, v, b3

    def sub_chunk_scores(q, k, v, b3, exact):
        b = b3[:, :LANE] + b3[:, LANE:2 * LANE] + b3[:, 2 * LANE:]
        b_last = b[0:1] if reverse else b[C - 1:C]
        rows = []
        worst = jnp.zeros_like(b_last)
        k_bf = k.astype(BF16)
        for i in range(0 if exact else nsub):
            if reverse:
                ref_i = b[(i + 1) * c:(i + 1) * c + 1] if i < nsub - 1 else jnp.zeros_like(b_last)
                end_i = b[i * c:i * c + 1]
            else:
                ref_i = b[i * c - 1:i * c] if i > 0 else jnp.zeros_like(b_last)
                end_i = b[(i + 1) * c - 1:(i + 1) * c]
            worst = jnp.minimum(worst, end_i - ref_i)
            blk = slice(i * c, (i + 1) * c)
            q_i = (q[blk] * jnp.exp2(b[blk] - ref_i)).astype(BF16)
            if reverse:
                k_i = (k[i * c:] * jnp.exp2(ref_i - b[i * c:])).astype(BF16)
                k_i = jnp.concatenate([k_bf[:i * c], k_i], axis=0) if i > 0 else k_i
            else:
                k_i = (k[:(i + 1) * c] * jnp.exp2(ref_i - b[:(i + 1) * c])).astype(BF16)
                k_i = jnp.concatenate([k_i, k_bf[(i + 1) * c:]], axis=0) if i < nsub - 1 else k_i
            rows.append(lax.dot_general(q_i, k_i, (((1,), (1,)), ((), ())), preferred_element_type=F32))
        q_in = (q * jnp.exp2(b)).astype(BF16)
        k_end = (k * jnp.exp2(b_last - b)).astype(BF16)
        return (rows, v), (q, k, b, v), worst, (q_in, k_end, jnp.exp2(b_last), v)

    def intra(rows, v):
        scores = jnp.where(tri, jnp.concatenate(rows, axis=0), 0.0).astype(BF16)
        return jnp.dot(scores, v, preferred_element_type=F32)

    def intra_exact(q, k, b, v):
        pair_ref[0] = b
        pair_ref[1] = k
        pair_ref[2] = v.astype(F32)
        t_idx = lax.broadcasted_iota(jnp.int32, (C, 1), 0)

        def source(s, acc):
            b_s = pair_ref[0, pl.ds(s, 1), :]
            w = q * pair_ref[1, pl.ds(s, 1), :] * jnp.exp2(jnp.minimum(b - b_s, 0.0))
            score = jnp.sum(w, axis=1, keepdims=True)
            seen = (t_idx <= s) if reverse else (t_idx >= s)
            return acc + jnp.where(seen, score, 0.0) * pair_ref[2, pl.ds(s, 1), :]

        return lax.fori_loop(0, C, source, jnp.zeros((C, HEAD_DIM), F32))

    def run_chunks(ci, exact):
        offs = []
        for u in range(U):
            cu = ci * U + u
            offs.append(pl.multiple_of(((nch - 1 - cu) if reverse else cu) * C, C))
        parts = [gates(off) for off in offs]
        parts = [sub_chunk_scores(*p, exact) for p in parts]
        worst = parts[0][2]
        for p in parts[1:]:
            worst = jnp.minimum(worst, p[2])
        o_intras = [intra_exact(*p[1]) if exact else intra(*p[0]) for p in parts]
        adds = [lax.dot_general(v, k_end, (((0,), (0,)), ((), ())), preferred_element_type=F32)
                for _, k_end, _, v in (p[3] for p in parts)]
        state = state_ref[...]
        for off, o_intra, add, (q_in, _, decay, _) in zip(offs, o_intras, adds, (p[3] for p in parts)):
            o = o_intra + lax.dot_general(q_in, state.astype(BF16), (((1,), (1,)), ((), ())),
                                          preferred_element_type=F32)
            state = state * decay + add
            if reverse:
                tot = o + ofwd_ref[pl.ds(off, C), :]
                y = tot * lax.rsqrt(jnp.mean(tot * tot, axis=-1, keepdims=True) + EPS) * gain_ref[...]
                gt = gate_ref[pl.ds(off, C), :]
                o_ref[pl.ds(off, C), :] = (y * gt * (1.0 / (1.0 + jnp.exp(-gt)))).astype(o_ref.dtype)
            else:
                o_ref[pl.ds(off, C), :] = o
        state_ref[...] = state
        return jnp.min(worst)

    def chunks(ci, carry):
        prev_ref[...] = state_ref[...]
        lowest = run_chunks(ci, exact=False)

        @pl.when(jnp.logical_not(lowest >= -HGRN_MAX_SUB_DECAY))
        def _():
            state_ref[...] = prev_ref[...]
            run_chunks(ci, exact=True)

        return carry

    lax.fori_loop(0, nch // U, chunks, 0)


def _hgrn_direction(proj, lb, reverse, ofwd=None, gain=None):
    B, S, _ = proj.shape
    tb = min(HGRN_BLOCK, S)
    nblk = S // tb
    blk = (lambda j: nblk - 1 - j) if reverse else (lambda j: j)

    def col_spec(col0):
        return pl.BlockSpec((None, tb, LANE), lambda b, h, j: (b, blk(j), col0 // LANE + h))

    z_col = COL_ZB if reverse else COL_ZF
    lb_row = 1 if reverse else 0
    in_specs = [col_spec(COL_Q), col_spec(z_col), col_spec(COL_I),
                pl.BlockSpec((None, 1, LANE), lambda b, h, j: (lb_row, 0, h))]
    args = [proj, proj, proj, lb]
    out_spec = pl.BlockSpec((None, tb, LANE), lambda b, h, j: (b, blk(j), h))
    if reverse:
        in_specs += [col_spec(0), col_spec(COL_GA), pl.BlockSpec((1, LANE), lambda b, h, j: (0, 0))]
        args += [ofwd, proj, gain]
        out_dtype = BF16
    else:
        out_dtype = F32
    return pl.pallas_call(
        functools.partial(_hgrn_kernel, reverse=reverse, tb=tb),
        grid=(B, HEADS, nblk),
        in_specs=in_specs,
        out_specs=out_spec,
        out_shape=jax.ShapeDtypeStruct((B, S, E_HGRN), out_dtype),
        scratch_shapes=[pltpu.VMEM((HEAD_DIM, HEAD_DIM), F32), pltpu.VMEM((HEAD_DIM, HEAD_DIM), F32),
                        pltpu.VMEM((3, HGRN_CHUNK, HEAD_DIM), F32)],
        compiler_params=_cparams(("parallel", "parallel", "arbitrary")),
        name="hgrn_bwd" if reverse else "hgrn_fwd",
    )(*args)


def _mla_prep_kernel(p_ref, gq_ref, gkv_ref, wq_ref, wkv_ref, cos_ref, sin_ref,
                     qt_ref, k_ref, vt_ref, knorm_ref):
    def rms(x, gain):
        return (x * lax.rsqrt(jnp.mean(x * x, axis=-1, keepdims=True) + EPS) * gain).astype(BF16)

    blk = p_ref[...]
    cos = cos_ref[...]
    sin = sin_ref[...]
    c_q = rms(blk[:, :Q_LORA], gq_ref[...])
    c_kv = rms(blk[:, Q_LORA:Q_LORA + KV_LORA], gkv_ref[...])
    rope_k = blk[:, Q_LORA + KV_LORA:Q_LORA + KV_LORA + LANE]
    rope_k_swapped = blk[:, Q_LORA + KV_LORA + LANE:]
    k_rot = (rope_k * cos + rope_k_swapped * sin).astype(BF16)
    rot_sq = jnp.sum(k_rot.astype(F32) * k_rot.astype(F32), axis=1, keepdims=True)

    q_all = jnp.dot(c_q, wq_ref[...], preferred_element_type=F32)
    kv = jnp.dot(c_kv, wkv_ref[...], preferred_element_type=F32)
    for h in range(HEADS):
        base = 3 * LANE * h
        q_nope = q_all[:, base:base + LANE]
        q_rot = q_all[:, base + LANE:base + 2 * LANE] * cos + q_all[:, base + 2 * LANE:base + 3 * LANE] * sin
        q_h = jnp.concatenate([q_nope, q_rot], axis=1) * (ATTN_SCALE * LOG2E)
        qt_ref[h * QK_PAD:(h + 1) * QK_PAD, :] = q_h.T.astype(BF16)
        k_nope = kv[:, h * LANE:(h + 1) * LANE].astype(BF16)
        k_ref[:, h * QK_PAD:h * QK_PAD + LANE] = k_nope
        k_ref[:, h * QK_PAD + LANE:(h + 1) * QK_PAD] = k_rot
        sq = jnp.sum(k_nope.astype(F32) * k_nope.astype(F32), axis=1, keepdims=True) + rot_sq
        knorm_ref[h:h + 1, :] = jnp.broadcast_to(jnp.max(sq, axis=0, keepdims=True), (1, LANE))
        vt_ref[h, 0] = kv[:, E_MLA + h * LANE:E_MLA + (h + 1) * LANE].T.astype(BF16)


def _mla_prep(proj, gq, gkv, wq, wkv, cos, sin, tm):
    B, S, _ = proj.shape
    ns = S // tm
    full = lambda shape: pl.BlockSpec(shape, lambda b, i: (0,) * len(shape))
    return pl.pallas_call(
        _mla_prep_kernel,
        grid=(B, ns),
        in_specs=[
            pl.BlockSpec((None, tm, 1024), lambda b, i: (b, i, COL_MISC // 1024)),
            full((1, Q_LORA)), full((1, KV_LORA)),
            full((Q_LORA, HEADS * 3 * LANE)), full((KV_LORA, 2 * E_MLA)),
            pl.BlockSpec((tm, LANE), lambda b, i: (i, 0)),
            pl.BlockSpec((tm, LANE), lambda b, i: (i, 0)),
        ],
        out_specs=[
            pl.BlockSpec((None, HEADS * QK_PAD, tm), lambda b, i: (b, 0, i)),
            pl.BlockSpec((None, tm, HEADS * QK_PAD), lambda b, i: (b, i, 0)),
            pl.BlockSpec((None, HEADS, 1, HEAD_DIM, tm), lambda b, i: (b, 0, i, 0, 0)),
            pl.BlockSpec((None, None, HEADS, LANE), lambda b, i: (b, i, 0, 0)),
        ],
        out_shape=[
            jax.ShapeDtypeStruct((B, HEADS * QK_PAD, S), BF16),
            jax.ShapeDtypeStruct((B, S, HEADS * QK_PAD), BF16),
            jax.ShapeDtypeStruct((B, HEADS, ns, HEAD_DIM, tm), BF16),
            jax.ShapeDtypeStruct((B, ns, HEADS, LANE), F32),
        ],
        compiler_params=_cparams(("parallel", "parallel")),
        name="mla_prep",
    )(proj, gq, gkv, wq, wkv, cos, sin)


def _attn_kernel(qt_ref, k_ref, vt_ref, knorm_ref, gate_ref, o_ref, m_ref, l_ref, acc_ref,
                 s0_ref, s1_ref, *, tk):
    nkv = k_ref.shape[0] // tk
    tq = qt_ref.shape[1]
    groups = [slice(g * ATTN_GROUP, (g + 1) * ATTN_GROUP) for g in range(tq // ATTN_GROUP)]

    def key_tile(j):
        return k_ref[pl.ds(pl.multiple_of(j * tk, tk), tk), :]

    k_max = jnp.sqrt(jnp.max(knorm_ref[...], axis=0, keepdims=True))[:, 0:1]
    q = qt_ref[...].astype(F32)
    m_ref[...] = jnp.sqrt(jnp.sum(q * q, axis=0, keepdims=True)) * k_max
    l_ref[...] = jnp.zeros_like(l_ref)
    acc_ref[...] = jnp.zeros_like(acc_ref)

    def scores_into(j, s_ref):
        k = key_tile(j)
        for sl in groups:
            s_ref[:, sl] = jnp.dot(k, qt_ref[:, sl], preferred_element_type=F32)

    bufs = (s0_ref, s1_ref)
    n = min(ATTN_UNROLL, nkv)

    def run_tiles(j0, last):
        for u in range(n):
            cur, nxt = bufs[u % 2], bufs[(u + 1) % 2]
            vt = vt_ref[j0 + u]
            k_next = key_tile(j0 + u + 1) if (u < n - 1 or not last) else None
            for sl in groups:
                if k_next is not None:
                    nxt[:, sl] = jnp.dot(k_next, qt_ref[:, sl], preferred_element_type=F32)
                p = jnp.exp2(cur[:, sl] - m_ref[:, sl])
                l_ref[:, sl] += jnp.sum(p, axis=0, keepdims=True)
                acc_ref[:, sl] += jnp.dot(vt, p.astype(BF16), preferred_element_type=F32)

    scores_into(0, s0_ref)

    def body(jj, carry):
        run_tiles(jj * n, False)
        return carry

    lax.fori_loop(0, nkv // n - 1, body, 0)
    run_tiles(nkv - n, True)

    def finish():
        out = (acc_ref[...] / l_ref[...]).T
        gt = gate_ref[...]
        o_ref[...] = (out * gt * (1.0 / (1.0 + jnp.exp(-gt)))).astype(o_ref.dtype)

    finish()

    @pl.when(jnp.logical_not(jnp.min(l_ref[...]) >= ATTN_MIN_ROW_SUM))
    def _():
        m_ref[...] = jnp.full_like(m_ref, -jnp.inf)
        l_ref[...] = jnp.zeros_like(l_ref)
        acc_ref[...] = jnp.zeros_like(acc_ref)

        def online(j, carry):
            scores_into(j, s0_ref)
            vt = vt_ref[j]
            for sl in groups:
                st = s0_ref[:, sl]
                m_prev = m_ref[:, sl]
                m_new = jnp.maximum(m_prev, jnp.max(st, axis=0, keepdims=True))
                alpha = jnp.exp2(m_prev - m_new)
                p = jnp.exp2(st - m_new)
                l_ref[:, sl] = alpha * l_ref[:, sl] + jnp.sum(p, axis=0, keepdims=True)
                acc_ref[:, sl] = alpha * acc_ref[:, sl] + jnp.dot(vt, p.astype(BF16),
                                                                  preferred_element_type=F32)
                m_ref[:, sl] = m_new
            return carry

        lax.fori_loop(0, nkv, online, 0)
        finish()


def _attention(qt, k, vt, knorm, proj, tq, tk):
    B, S, _ = k.shape
    return pl.pallas_call(
        functools.partial(_attn_kernel, tk=tk),
        grid=(B, HEADS, S // tq),
        in_specs=[
            pl.BlockSpec((None, QK_PAD, tq), lambda b, h, i: (b, h, i)),
            pl.BlockSpec((None, S, QK_PAD), lambda b, h, i: (b, 0, h)),
            pl.BlockSpec((None, None, S // tk, HEAD_DIM, tk), lambda b, h, i: (b, h, 0, 0, 0)),
            pl.BlockSpec((None, None, S // tk, LANE), lambda b, h, i: (b, h, 0, 0)),
            pl.BlockSpec((None, tq, LANE), lambda b, h, i: (b, i, COL_GB // LANE + h)),
        ],
        out_specs=pl.BlockSpec((None, tq, LANE), lambda b, h, i: (b, i, h)),
        out_shape=jax.ShapeDtypeStruct((B, S, E_MLA), BF16),
        scratch_shapes=[pltpu.VMEM((1, tq), F32), pltpu.VMEM((1, tq), F32),
                        pltpu.VMEM((HEAD_DIM, tq), F32),
                        pltpu.VMEM((tk, tq), F32), pltpu.VMEM((tk, tq), F32)],
        compiler_params=_cparams(("parallel", "parallel", "arbitrary")),
        name="attention",
    )(qt, k, vt, knorm, proj)


def _merge_kernel(x_ref, ya_ref, yb_ref, ma_ref, mb_ref, wpa_ref, wpb_ref, wout_ref, gain_ref, o_ref,
                  *, last_layer):
    def sigmoid(t):
        return 1.0 / (1.0 + jnp.exp(-t))

    y_a = jnp.dot(ya_ref[...], wpa_ref[...], preferred_element_type=F32)
    y_b = jnp.dot(yb_ref[...], wpb_ref[...], preferred_element_type=F32)
    merged = sigmoid(ma_ref[...]) * y_a + sigmoid(mb_ref[...]) * y_b
    x = x_ref[...] + jnp.dot(merged.astype(BF16), wout_ref[...], preferred_element_type=F32)
    if last_layer:
        x = x * lax.rsqrt(jnp.mean(x * x, axis=-1, keepdims=True) + EPS) * gain_ref[...]
    o_ref[...] = x


def _merge(x, ya, yb, proj, wpa, wpb, wout, final_gain, last_layer):
    T = x.shape[0]
    tm = min(MERGE_ROWS, T)
    row = lambda width, colblk: pl.BlockSpec((tm, width), lambda i: (i, colblk))
    full = lambda shape: pl.BlockSpec(shape, lambda i: (0, 0))
    return pl.pallas_call(
        functools.partial(_merge_kernel, last_layer=last_layer),
        grid=(T // tm,),
        in_specs=[row(D_MODEL, 0), row(E_HGRN, 0), row(E_MLA, 0), row(D_MODEL, 0), row(D_MODEL, 1),
                  full((E_HGRN, D_MODEL)), full((E_MLA, D_MODEL)), full((D_MODEL, D_MODEL)),
                  full((1, D_MODEL))],
        out_specs=row(D_MODEL, 0),
        out_shape=jax.ShapeDtypeStruct((T, D_MODEL), F32),
        compiler_params=_cparams(("parallel",)),
        name="merge",
    )(x, ya, yb, proj, proj, wpa, wpb, wout, final_gain)


def _prep_layer_weights(w_uq, w_ukv, w_pa, w_pb, w_out):
    wq = w_uq.reshape(Q_LORA, HEADS, HEAD_DIM + QK_ROPE)
    r1 = wq[:, :, HEAD_DIM:HEAD_DIM + ROPE_HALF]
    r2 = wq[:, :, HEAD_DIM + ROPE_HALF:]
    zq = jnp.zeros((Q_LORA, HEADS, LANE - QK_ROPE), w_uq.dtype)
    wq = jnp.concatenate([wq[:, :, :HEAD_DIM], r1, r2, zq, r2, r1, zq], axis=2)
    wq = wq.reshape(Q_LORA, HEADS * 3 * LANE).astype(BF16)

    wkv = w_ukv.reshape(KV_LORA, HEADS, 2 * HEAD_DIM)
    wkv = jnp.concatenate([wkv[:, :, :HEAD_DIM].reshape(KV_LORA, E_MLA),
                           wkv[:, :, HEAD_DIM:].reshape(KV_LORA, E_MLA)], axis=1).astype(BF16)
    return wq, wkv, w_pa.astype(BF16), w_pb.astype(BF16), w_out.astype(BF16)


def _rope_tables(S):
    inv_freq = ROPE_BASE ** (-jnp.arange(ROPE_HALF, dtype=F32) / ROPE_HALF)
    ang = jnp.arange(S, dtype=F32)[:, None] * inv_freq[None, :]
    cos, sin = jnp.cos(ang), jnp.sin(ang)
    zeros = jnp.zeros((S, LANE - QK_ROPE), F32)
    return (jnp.concatenate([cos, cos, zeros], axis=1),
            jnp.concatenate([-sin, sin, zeros], axis=1))


def _trunk(x, w_proj, layers, lbs, norm_gain, hgrn_norm_gain, q_norm_gain, kv_norm_gain, final_norm_gain):
    B, S, _ = x.shape
    T = B * S
    cos, sin = _rope_tables(S)
    tk = min(ATTN_KEY_TILE, S // 2)
    tq = min(ATTN_QUERY_TILE, S)
    xf = x.reshape(T, D_MODEL)
    for l in range(DEPTH):
        wq, wkv, wpa, wpb, wout = layers[l]
        proj = _inproj(xf, norm_gain[l].reshape(1, D_MODEL), w_proj, l)
        proj3 = proj.reshape(B, S, PROJ_COLS)
        lb = lbs[l].reshape(2, 1, E_HGRN)
        o_fwd = _hgrn_direction(proj3, lb, reverse=False)
        ya = _hgrn_direction(proj3, lb, reverse=True, ofwd=o_fwd,
                             gain=hgrn_norm_gain[l].reshape(1, HEAD_DIM))
        qt, k, vt, knorm = _mla_prep(proj3, q_norm_gain[l].reshape(1, Q_LORA),
                              kv_norm_gain[l].reshape(1, KV_LORA), wq, wkv, cos, sin, tk)
        yb = _attention(qt, k, vt, knorm.transpose(0, 2, 1, 3), proj3, tq, tk)
        xf = _merge(xf, ya.reshape(T, E_HGRN), yb.reshape(T, E_MLA), proj, wpa, wpb, wout,
                    final_norm_gain.reshape(1, D_MODEL), last_layer=(l == DEPTH - 1))
    return xf.reshape(B, S, D_MODEL)


def kernel(x_prompt, x_sample, norm_gain, w_in, lb_logits, hgrn_norm_gain, q_norm_gain, w_uq,
           kv_norm_gain, w_ukv, w_pa, w_pb, w_out, final_norm_gain):
    lbs = _lower_bounds(lb_logits)
    layers = [_prep_layer_weights(w_uq[l], w_ukv[l], w_pa[l], w_pb[l], w_out[l]) for l in range(DEPTH)]
    args = (_pack_w_in(w_in), layers, lbs, norm_gain, hgrn_norm_gain, q_norm_gain, kv_norm_gain,
            final_norm_gain)
    return (_trunk(x_prompt, *args), _trunk(x_sample, *args))
```

```python
import functools

import jax
import jax.numpy as jnp
import numpy as np
from jax import lax
from jax.experimental import pallas as pl
from jax.experimental.pallas import tpu as pltpu

F32 = jnp.float32
BF16 = jnp.bfloat16

D_MODEL = 2048
DEPTH = 4
E_HGRN = 1024
HEADS = 8
HEAD_DIM = 128
QK_ROPE = 64
ROPE_HALF = QK_ROPE // 2
Q_LORA = 512
KV_LORA = 256
E_MLA = 1024
ROPE_BASE = 10000.0
EPS = 1e-6
ATTN_SCALE = (HEAD_DIM + QK_ROPE) ** -0.5

PROJ_COLS = 11264
COL_Q, COL_ZF, COL_ZB, COL_I, COL_GA = 4096, 5120, 6144, 7168, 8192
COL_MISC, COL_GB = 9216, 10240
LANE = 128
QK_PAD = 2 * HEAD_DIM

HGRN_CHUNK = 128
HGRN_SUB = 32
HGRN_UNROLL = 16
HGRN_MAX_SUB_DECAY = 120.0
ATTN_GROUP = 256
ATTN_UNROLL = 8
ATTN_MIN_ROW_SUM = 2.0 ** -80
LOG2E = 1.4426950408889634

PACK_ROWS = 256
INPROJ_ROWS = 1024
INPROJ_COL_TILE = 1024
HGRN_BLOCK = 2048
ATTN_QUERY_TILE = 1024
ATTN_KEY_TILE = 512
MERGE_ROWS = 256
VMEM_V7X = 64 * 1024 * 1024
VMEM_LIMIT = VMEM_V7X - 8 * 1024 * 1024


def _cparams(sem):
    return pltpu.CompilerParams(dimension_semantics=sem, vmem_limit_bytes=VMEM_LIMIT)


def _lower_bound_kernel(x_ref, o_ref):
    x = x_ref[...]
    e = jnp.exp(x - jnp.max(x, axis=0, keepdims=True))
    p = e / jnp.sum(e, axis=0, keepdims=True)
    c = p[0:1]
    rows = [c]
    for l in range(1, DEPTH):
        c = c + p[l:l + 1]
        rows.append(c)
    o_ref[...] = jnp.concatenate([r - rows[0] for r in rows], axis=0)


def _lower_bounds(lb_logits):
    x = lb_logits.reshape(DEPTH, 2 * E_HGRN)
    return pl.pallas_call(
        _lower_bound_kernel,
        out_shape=jax.ShapeDtypeStruct((DEPTH, 2 * E_HGRN), F32),
        name="lower_bounds",
    )(x)


IN_OFFSETS = tuple(int(v) for v in np.cumsum(
    (0, E_HGRN, E_HGRN, E_HGRN, E_HGRN, E_HGRN, Q_LORA, KV_LORA, QK_ROPE, E_MLA, D_MODEL, D_MODEL)))
IN_COLS = IN_OFFSETS[-1]


def _pack_w_in_kernel(w_ref, o_ref):
    o = IN_OFFSETS

    def put(dst, first, last):
        o_ref[:, dst:dst + o[last] - o[first]] = w_ref[:, o[first]:o[last]].astype(BF16)

    put(0, 9, 11)
    put(COL_Q, 0, 5)
    put(COL_MISC, 5, 7)
    rope = w_ref[:, o[7]:o[8]]
    zeros = jnp.zeros((rope.shape[0], LANE - QK_ROPE), F32)
    rope_at = COL_MISC + Q_LORA + KV_LORA
    o_ref[:, rope_at:rope_at + LANE] = jnp.concatenate([rope, zeros], axis=1).astype(BF16)
    o_ref[:, rope_at + LANE:rope_at + 2 * LANE] = jnp.concatenate(
        [rope[:, ROPE_HALF:], rope[:, :ROPE_HALF], zeros], axis=1).astype(BF16)
    put(COL_GB, 8, 9)


def _pack_w_in(w_in):
    tr = PACK_ROWS
    return pl.pallas_call(
        _pack_w_in_kernel,
        grid=(DEPTH, D_MODEL // tr),
        in_specs=[pl.BlockSpec((None, tr, IN_COLS), lambda l, i: (l, i, 0))],
        out_specs=pl.BlockSpec((None, tr, PROJ_COLS), lambda l, i: (l, i, 0)),
        out_shape=jax.ShapeDtypeStruct((DEPTH, D_MODEL, PROJ_COLS), BF16),
        compiler_params=_cparams(("parallel", "parallel")),
        name="pack_w_in",
    )(w_in)


def _inproj_kernel(x_ref, g_ref, w_ref, o_ref, h_ref):
    @pl.when(pl.program_id(1) == 0)
    def _():
        x = x_ref[...]
        ms = jnp.mean(x * x, axis=-1, keepdims=True)
        h_ref[...] = (x * lax.rsqrt(ms + EPS) * g_ref[...]).astype(BF16)

    o_ref[...] = jnp.dot(h_ref[...], w_ref[...], preferred_element_type=F32)


def _inproj(x, gain, w, layer):
    T = x.shape[0]
    tm, tn = min(INPROJ_ROWS, T), INPROJ_COL_TILE
    return pl.pallas_call(
        _inproj_kernel,
        grid=(T // tm, PROJ_COLS // tn),
        in_specs=[
            pl.BlockSpec((tm, D_MODEL), lambda i, j: (i, 0)),
            pl.BlockSpec((1, D_MODEL), lambda i, j: (0, 0)),
            pl.BlockSpec((None, D_MODEL, tn), lambda i, j: (layer, 0, j)),
        ],
        out_specs=pl.BlockSpec((tm, tn), lambda i, j: (i, j)),
        out_shape=jax.ShapeDtypeStruct((T, PROJ_COLS), F32),
        scratch_shapes=[pltpu.VMEM((tm, D_MODEL), BF16)],
        compiler_params=_cparams(("parallel", "arbitrary")),
        name="inproj",
    )(x, gain, w)


def _hgrn_kernel(*refs, reverse, tb):
    if reverse:
        q_ref, z_ref, v_ref, lb_ref, ofwd_ref, gate_ref, gain_ref, o_ref, state_ref, prev_ref, pair_ref = refs
    else:
        q_ref, z_ref, v_ref, lb_ref, o_ref, state_ref, prev_ref, pair_ref = refs
    C, c = HGRN_CHUNK, HGRN_SUB
    nsub = C // c
    nch = tb // C
    U = min(HGRN_UNROLL, nch)

    @pl.when(pl.program_id(2) == 0)
    def _():
        state_ref[...] = jnp.zeros_like(state_ref)

    lb = lb_ref[...]
    log_lb = jnp.log(lb) * LOG2E
    log_1m_lb = jnp.log1p(-lb) * LOG2E
    one_m_lb = 1.0 - lb

    row = lax.broadcasted_iota(jnp.int32, (C, C), 0)
    col = lax.broadcasted_iota(jnp.int32, (C, C), 1)
    tri = (col >= row) if reverse else (col <= row)
    tri_b = tri.astype(BF16)

    def gates(off):
        z = z_ref[pl.ds(off, C), :]
        q_raw = q_ref[pl.ds(off, C), :]
        v = v_ref[pl.ds(off, C), :].astype(BF16)

        e = jnp.exp2(jnp.abs(z) * -LOG2E)
        r = 1.0 / (1.0 + e)
        log_sig = jnp.minimum(z, 0.0) * LOG2E - jnp.log2(1.0 + e)
        k = one_m_lb * jnp.where(z >= 0.0, e * r, r)
        t = log_1m_lb + log_sig
        g = jnp.maximum(log_lb, t) + jnp.log2(1.0 + jnp.exp2(-jnp.abs(log_lb - t)))
        q = q_raw * (1.0 / (1.0 + jnp.exp(-q_raw)))

        g_hi = g.astype(BF16)
        rem = g - g_hi.astype(F32)
        g_mid = rem.astype(BF16)
        g_lo = (rem - g_mid.astype(F32)).astype(BF16)
        b3 = jnp.dot(tri_b, jnp.concatenate([g_hi, g_mid, g_lo], axis=1), preferred_element_type=F32)
        return q, k, v, b3

    def sub_chunk_scores(q, k, v, b3, exact):
        b = b3[:, :LANE] + b3[:, LANE:2 * LANE] + b3[:, 2 * LANE:]
        b_last = b[0:1] if reverse else b[C - 1:C]
        rows = []
        worst = jnp.zeros_like(b_last)
        k_bf = k.astype(BF16)
        for i in range(0 if exact else nsub):
            if reverse:
                ref_i = b[(i + 1) * c:(i + 1) * c + 1] if i < nsub - 1 else jnp.zeros_like(b_last)
                end_i = b[i * c:i * c + 1]
            else:
                ref_i = b[i * c - 1:i * c] if i > 0 else jnp.zeros_like(b_last)
                end_i = b[(i + 1) * c - 1:(i + 1) * c]
            worst = jnp.minimum(worst, end_i - ref_i)
            blk = slice(i * c, (i + 1) * c)
            q_i = (q[blk] * jnp.exp2(b[blk] - ref_i)).astype(BF16)
            if reverse:
                k_i = (k[i * c:] * jnp.exp2(ref_i - b[i * c:])).astype(BF16)
                k_i = jnp.concatenate([k_bf[:i * c], k_i], axis=0) if i > 0 else k_i
            else:
                k_i = (k[:(i + 1) * c] * jnp.exp2(ref_i - b[:(i + 1) * c])).astype(BF16)
                k_i = jnp.concatenate([k_i, k_bf[(i + 1) * c:]], axis=0) if i < nsub - 1 else k_i
            rows.append(lax.dot_general(q_i, k_i, (((1,), (1,)), ((), ())), preferred_element_type=F32))
        q_in = (q * jnp.exp2(b)).astype(BF16)
        k_end = (k * jnp.exp2(b_last - b)).astype(BF16)
        return (rows, v), (q, k, b, v), worst, (q_in, k_end, jnp.exp2(b_last), v)

    def intra(rows, v):
        scores = jnp.where(tri, jnp.concatenate(rows, axis=0), 0.0).astype(BF16)
        return jnp.dot(scores, v, preferred_element_type=F32)

    def intra_exact(q, k, b, v):
        pair_ref[0] = b
        pair_ref[1] = k
        pair_ref[2] = v.astype(F32)
        t_idx = lax.broadcasted_iota(jnp.int32, (C, 1), 0)

        def source(s, acc):
            b_s = pair_ref[0, pl.ds(s, 1), :]
            w = q * pair_ref[1, pl.ds(s, 1), :] * jnp.exp2(jnp.minimum(b - b_s, 0.0))
            score = jnp.sum(w, axis=1, keepdims=True)
            seen = (t_idx <= s) if reverse else (t_idx >= s)
            return acc + jnp.where(seen, score, 0.0) * pair_ref[2, pl.ds(s, 1), :]

        return lax.fori_loop(0, C, source, jnp.zeros((C, HEAD_DIM), F32))

    def run_chunks(ci, exact):
        offs = []
        for u in range(U):
            cu = ci * U + u
            offs.append(pl.multiple_of(((nch - 1 - cu) if reverse else cu) * C, C))
        parts = [gates(off) for off in offs]
        parts = [sub_chunk_scores(*p, exact) for p in parts]
        worst = parts[0][2]
        for p in parts[1:]:
            worst = jnp.minimum(worst, p[2])
        o_intras = [intra_exact(*p[1]) if exact else intra(*p[0]) for p in parts]
        adds = [lax.dot_general(v, k_end, (((0,), (0,)), ((), ())), preferred_element_type=F32)
                for _, k_end, _, v in (p[3] for p in parts)]
        state = state_ref[...]
        for off, o_intra, add, (q_in, _, decay, _) in zip(offs, o_intras, adds, (p[3] for p in parts)):
            o = o_intra + lax.dot_general(q_in, state.astype(BF16), (((1,), (1,)), ((), ())),
                                          preferred_element_type=F32)
            state = state * decay + add
            if reverse:
                tot = o + ofwd_ref[pl.ds(off, C), :]
                y = tot * lax.rsqrt(jnp.mean(tot * tot, axis=-1, keepdims=True) + EPS) * gain_ref[...]
                gt = gate_ref[pl.ds(off, C), :]
                o_ref[pl.ds(off, C), :] = (y * gt * (1.0 / (1.0 + jnp.exp(-gt)))).astype(o_ref.dtype)
            else:
                o_ref[pl.ds(off, C), :] = o
        state_ref[...] = state
        return jnp.min(worst)

    def chunks(ci, carry):
        prev_ref[...] = state_ref[...]
        lowest = run_chunks(ci, exact=False)

        @pl.when(jnp.logical_not(lowest >= -HGRN_MAX_SUB_DECAY))
        def _():
            state_ref[...] = prev_ref[...]
            run_chunks(ci, exact=True)

        return carry

    lax.fori_loop(0, nch // U, chunks, 0)


def _hgrn_direction(proj, lb, reverse, ofwd=None, gain=None):
    B, S, _ = proj.shape
    tb = min(HGRN_BLOCK, S)
    nblk = S // tb
    blk = (lambda j: nblk - 1 - j) if reverse else (lambda j: j)

    def col_spec(col0):
        return pl.BlockSpec((None, tb, LANE), lambda b, h, j: (b, blk(j), col0 // LANE + h))

    z_col = COL_ZB if reverse else COL_ZF
    lb_row = 1 if reverse else 0
    in_specs = [col_spec(COL_Q), col_spec(z_col), col_spec(COL_I),
                pl.BlockSpec((None, 1, LANE), lambda b, h, j: (lb_row, 0, h))]
    args = [proj, proj, proj, lb]
    out_spec = pl.BlockSpec((None, tb, LANE), lambda b, h, j: (b, blk(j), h))
    if reverse:
        in_specs += [col_spec(0), col_spec(COL_GA), pl.BlockSpec((1, LANE), lambda b, h, j: (0, 0))]
        args += [ofwd, proj, gain]
        out_dtype = BF16
    else:
        out_dtype = F32
    return pl.pallas_call(
        functools.partial(_hgrn_kernel, reverse=reverse, tb=tb),
        grid=(B, HEADS, nblk),
        in_specs=in_specs,
        out_specs=out_spec,
        out_shape=jax.ShapeDtypeStruct((B, S, E_HGRN), out_dtype),
        scratch_shapes=[pltpu.VMEM((HEAD_DIM, HEAD_DIM), F32), pltpu.VMEM((HEAD_DIM, HEAD_DIM), F32),
                        pltpu.VMEM((3, HGRN_CHUNK, HEAD_DIM), F32)],
        compiler_params=_cparams(("parallel", "parallel", "arbitrary")),
        name="hgrn_bwd" if reverse else "hgrn_fwd",
    )(*args)


def _mla_prep_kernel(p_ref, gq_ref, gkv_ref, wq_ref, wkv_ref, cos_ref, sin_ref,
                     qt_ref, k_ref, vt_ref, knorm_ref):
    def rms(x, gain):
        return (x * lax.rsqrt(jnp.mean(x * x, axis=-1, keepdims=True) + EPS) * gain).astype(BF16)

    blk = p_ref[...]
    cos = cos_ref[...]
    sin = sin_ref[...]
    c_q = rms(blk[:, :Q_LORA], gq_ref[...])
    c_kv = rms(blk[:, Q_LORA:Q_LORA + KV_LORA], gkv_ref[...])
    rope_k = blk[:, Q_LORA + KV_LORA:Q_LORA + KV_LORA + LANE]
    rope_k_swapped = blk[:, Q_LORA + KV_LORA + LANE:]
    k_rot = (rope_k * cos + rope_k_swapped * sin).astype(BF16)
    rot_sq = jnp.sum(k_rot.astype(F32) * k_rot.astype(F32), axis=1, keepdims=True)

    q_all = jnp.dot(c_q, wq_ref[...], preferred_element_type=F32)
    kv = jnp.dot(c_kv, wkv_ref[...], preferred_element_type=F32)
    for h in range(HEADS):
        base = 3 * LANE * h
        q_nope = q_all[:, base:base + LANE]
        q_rot = q_all[:, base + LANE:base + 2 * LANE] * cos + q_all[:, base + 2 * LANE:base + 3 * LANE] * sin
        q_h = jnp.concatenate([q_nope, q_rot], axis=1) * (ATTN_SCALE * LOG2E)
        qt_ref[h * QK_PAD:(h + 1) * QK_PAD, :] = q_h.T.astype(BF16)
        k_nope = kv[:, h * LANE:(h + 1) * LANE].astype(BF16)
        k_ref[:, h * QK_PAD:h * QK_PAD + LANE] = k_nope
        k_ref[:, h * QK_PAD + LANE:(h + 1) * QK_PAD] = k_rot
        sq = jnp.sum(k_nope.astype(F32) * k_nope.astype(F32), axis=1, keepdims=True) + rot_sq
        knorm_ref[h:h + 1, :] = jnp.broadcast_to(jnp.max(sq, axis=0, keepdims=True), (1, LANE))
        vt_ref[h, 0] = kv[:, E_MLA + h * LANE:E_MLA + (h + 1) * LANE].T.astype(BF16)


def _mla_prep(proj, gq, gkv, wq, wkv, cos, sin, tm):
    B, S, _ = proj.shape
    ns = S // tm
    full = lambda shape: pl.BlockSpec(shape, lambda b, i: (0,) * len(shape))
    return pl.pallas_call(
        _mla_prep_kernel,
        grid=(B, ns),
        in_specs=[
            pl.BlockSpec((None, tm, 1024), lambda b, i: (b, i, COL_MISC // 1024)),
            full((1, Q_LORA)), full((1, KV_LORA)),
            full((Q_LORA, HEADS * 3 * LANE)), full((KV_LORA, 2 * E_MLA)),
            pl.BlockSpec((tm, LANE), lambda b, i: (i, 0)),
            pl.BlockSpec((tm, LANE), lambda b, i: (i, 0)),
        ],
        out_specs=[
            pl.BlockSpec((None, HEADS * QK_PAD, tm), lambda b, i: (b, 0, i)),
            pl.BlockSpec((None, tm, HEADS * QK_PAD), lambda b, i: (b, i, 0)),
            pl.BlockSpec((None, HEADS, 1, HEAD_DIM, tm), lambda b, i: (b, 0, i, 0, 0)),
            pl.BlockSpec((None, None, HEADS, LANE), lambda b, i: (b, i, 0, 0)),
        ],
        out_shape=[
            jax.ShapeDtypeStruct((B, HEADS * QK_PAD, S), BF16),
            jax.ShapeDtypeStruct((B, S, HEADS * QK_PAD), BF16),
            jax.ShapeDtypeStruct((B, HEADS, ns, HEAD_DIM, tm), BF16),
            jax.ShapeDtypeStruct((B, ns, HEADS, LANE), F32),
        ],
        compiler_params=_cparams(("parallel", "parallel")),
        name="mla_prep",
    )(proj, gq, gkv, wq, wkv, cos, sin)


def _attn_kernel(qt_ref, k_ref, vt_ref, knorm_ref, gate_ref, o_ref, m_ref, l_ref, acc_ref,
                 s0_ref, s1_ref, *, tk):
    nkv = k_ref.shape[0] // tk
    tq = qt_ref.shape[1]
    groups = [slice(g * ATTN_GROUP, (g + 1) * ATTN_GROUP) for g in range(tq // ATTN_GROUP)]

    def key_tile(j):
        return k_ref[pl.ds(pl.multiple_of(j * tk, tk), tk), :]

    k_max = jnp.sqrt(jnp.max(knorm_ref[...], axis=0, keepdims=True))[:, 0:1]
    q = qt_ref[...].astype(F32)
    m_ref[...] = jnp.sqrt(jnp.sum(q * q, axis=0, keepdims=True)) * k_max
    l_ref[...] = jnp.zeros_like(l_ref)
    acc_ref[...] = jnp.zeros_like(acc_ref)

    def scores_into(j, s_ref):
        k = key_tile(j)
        for sl in groups:
            s_ref[:, sl] = jnp.dot(k, qt_ref[:, sl], preferred_element_type=F32)

    bufs = (s0_ref, s1_ref)
    n = min(ATTN_UNROLL, nkv)

    def run_tiles(j0, last):
        for u in range(n):
            cur, nxt = bufs[u % 2], bufs[(u + 1) % 2]
            vt = vt_ref[j0 + u]
            k_next = key_tile(j0 + u + 1) if (u < n - 1 or not last) else None
            for sl in groups:
                if k_next is not None:
                    nxt[:, sl] = jnp.dot(k_next, qt_ref[:, sl], preferred_element_type=F32)
                p = jnp.exp2(cur[:, sl] - m_ref[:, sl])
                l_ref[:, sl] += jnp.sum(p, axis=0, keepdims=True)
                acc_ref[:, sl] += jnp.dot(vt, p.astype(BF16), preferred_element_type=F32)

    scores_into(0, s0_ref)

    def body(jj, carry):
        run_tiles(jj * n, False)
        return carry

    lax.fori_loop(0, nkv // n - 1, body, 0)
    run_tiles(nkv - n, True)

    def finish():
        out = (acc_ref[...] / l_ref[...]).T
        gt = gate_ref[...]
        o_ref[...] = (out * gt * (1.0 / (1.0 + jnp.exp(-gt)))).astype(o_ref.dtype)

    finish()

    @pl.when(jnp.logical_not(jnp.min(l_ref[...]) >= ATTN_MIN_ROW_SUM))
    def _():
        m_ref[...] = jnp.full_like(m_ref, -jnp.inf)
        l_ref[...] = jnp.zeros_like(l_ref)
        acc_ref[...] = jnp.zeros_like(acc_ref)

        def online(j, carry):
            scores_into(j, s0_ref)
            vt = vt_ref[j]
            for sl in groups:
                st = s0_ref[:, sl]
                m_prev = m_ref[:, sl]
                m_new = jnp.maximum(m_prev, jnp.max(st, axis=0, keepdims=True))
                alpha = jnp.exp2(m_prev - m_new)
                p = jnp.exp2(st - m_new)
                l_ref[:, sl] = alpha * l_ref[:, sl] + jnp.sum(p, axis=0, keepdims=True)
                acc_ref[:, sl] = alpha * acc_ref[:, sl] + jnp.dot(vt, p.astype(BF16),
                                                                  preferred_element_type=F32)
                m_ref[:, sl] = m_new
            return carry

        lax.fori_loop(0, nkv, online, 0)
        finish()


def _attention(qt, k, vt, knorm, proj, tq, tk):
    B, S, _ = k.shape
    return pl.pallas_call(
        functools.partial(_attn_kernel, tk=tk),
        grid=(B, HEADS, S // tq),
        in_specs=[
            pl.BlockSpec((None, QK_PAD, tq), lambda b, h, i: (b, h, i)),
            pl.BlockSpec((None, S, QK_PAD), lambda b, h, i: (b, 0, h)),
            pl.BlockSpec((None, None, S // tk, HEAD_DIM, tk), lambda b, h, i: (b, h, 0, 0, 0)),
            pl.BlockSpec((None, None, S // tk, LANE), lambda b, h, i: (b, h, 0, 0)),
            pl.BlockSpec((None, tq, LANE), lambda b, h, i: (b, i, COL_GB // LANE + h)),
        ],
        out_specs=pl.BlockSpec((None, tq, LANE), lambda b, h, i: (b, i, h)),
        out_shape=jax.ShapeDtypeStruct((B, S, E_MLA), BF16),
        scratch_shapes=[pltpu.VMEM((1, tq), F32), pltpu.VMEM((1, tq), F32),
                        pltpu.VMEM((HEAD_DIM, tq), F32),
                        pltpu.VMEM((tk, tq), F32), pltpu.VMEM((tk, tq), F32)],
        compiler_params=_cparams(("parallel", "parallel", "arbitrary")),
        name="attention",
    )(qt, k, vt, knorm, proj)


def _merge_kernel(x_ref, ya_ref, yb_ref, ma_ref, mb_ref, wpa_ref, wpb_ref, wout_ref, gain_ref, o_ref,
                  *, last_layer):
    def sigmoid(t):
        return 1.0 / (1.0 + jnp.exp(-t))

    y_a = jnp.dot(ya_ref[...], wpa_ref[...], preferred_element_type=F32)
    y_b = jnp.dot(yb_ref[...], wpb_ref[...], preferred_element_type=F32)
    merged = sigmoid(ma_ref[...]) * y_a + sigmoid(mb_ref[...]) * y_b
    x = x_ref[...] + jnp.dot(merged.astype(BF16), wout_ref[...], preferred_element_type=F32)
    if last_layer:
        x = x * lax.rsqrt(jnp.mean(x * x, axis=-1, keepdims=True) + EPS) * gain_ref[...]
    o_ref[...] = x


def _merge(x, ya, yb, proj, wpa, wpb, wout, final_gain, last_layer):
    T = x.shape[0]
    tm = min(MERGE_ROWS, T)
    row = lambda width, colblk: pl.BlockSpec((tm, width), lambda i: (i, colblk))
    full = lambda shape: pl.BlockSpec(shape, lambda i: (0, 0))
    return pl.pallas_call(
        functools.partial(_merge_kernel, last_layer=last_layer),
        grid=(T // tm,),
        in_specs=[row(D_MODEL, 0), row(E_HGRN, 0), row(E_MLA, 0), row(D_MODEL, 0), row(D_MODEL, 1),
                  full((E_HGRN, D_MODEL)), full((E_MLA, D_MODEL)), full((D_MODEL, D_MODEL)),
                  full((1, D_MODEL))],
        out_specs=row(D_MODEL, 0),
        out_shape=jax.ShapeDtypeStruct((T, D_MODEL), F32),
        compiler_params=_cparams(("parallel",)),
        name="merge",
    )(x, ya, yb, proj, proj, wpa, wpb, wout, final_gain)


def _prep_layer_weights(w_uq, w_ukv, w_pa, w_pb, w_out):
    wq = w_uq.reshape(Q_LORA, HEADS, HEAD_DIM + QK_ROPE)
    r1 = wq[:, :, HEAD_DIM:HEAD_DIM + ROPE_HALF]
    r2 = wq[:, :, HEAD_DIM + ROPE_HALF:]
    zq = jnp.zeros((Q_LORA, HEADS, LANE - QK_ROPE), w_uq.dtype)
    wq = jnp.concatenate([wq[:, :, :HEAD_DIM], r1, r2, zq, r2, r1, zq], axis=2)
    wq = wq.reshape(Q_LORA, HEADS * 3 * LANE).astype(BF16)

    wkv = w_ukv.reshape(KV_LORA, HEADS, 2 * HEAD_DIM)
    wkv = jnp.concatenate([wkv[:, :, :HEAD_DIM].reshape(KV_LORA, E_MLA),
                           wkv[:, :, HEAD_DIM:].reshape(KV_LORA, E_MLA)], axis=1).astype(BF16)
    return wq, wkv, w_pa.astype(BF16), w_pb.astype(BF16), w_out.astype(BF16)


def _rope_tables(S):
    inv_freq = ROPE_BASE ** (-jnp.arange(ROPE_HALF, dtype=F32) / ROPE_HALF)
    ang = jnp.arange(S, dtype=F32)[:, None] * inv_freq[None, :]
    cos, sin = jnp.cos(ang), jnp.sin(ang)
    zeros = jnp.zeros((S, LANE - QK_ROPE), F32)
    return (jnp.concatenate([cos, cos, zeros], axis=1),
            jnp.concatenate([-sin, sin, zeros], axis=1))


def _trunk(x, w_proj, layers, lbs, norm_gain, hgrn_norm_gain, q_norm_gain, kv_norm_gain, final_norm_gain):
    B, S, _ = x.shape
    T = B * S
    cos, sin = _rope_tables(S)
    tk = min(ATTN_KEY_TILE, S // 2)
    tq = min(ATTN_QUERY_TILE, S)
    xf = x.reshape(T, D_MODEL)
    for l in range(DEPTH):
        wq, wkv, wpa, wpb, wout = layers[l]
        proj = _inproj(xf, norm_gain[l].reshape(1, D_MODEL), w_proj, l)
        proj3 = proj.reshape(B, S, PROJ_COLS)
        lb = lbs[l].reshape(2, 1, E_HGRN)
        o_fwd = _hgrn_direction(proj3, lb, reverse=False)
        ya = _hgrn_direction(proj3, lb, reverse=True, ofwd=o_fwd,
                             gain=hgrn_norm_gain[l].reshape(1, HEAD_DIM))
        qt, k, vt, knorm = _mla_prep(proj3, q_norm_gain[l].reshape(1, Q_LORA),
                              kv_norm_gain[l].reshape(1, KV_LORA), wq, wkv, cos, sin, tk)
        yb = _attention(qt, k, vt, knorm.transpose(0, 2, 1, 3), proj3, tq, tk)
        xf = _merge(xf, ya.reshape(T, E_HGRN), yb.reshape(T, E_MLA), proj, wpa, wpb, wout,
                    final_norm_gain.reshape(1, D_MODEL), last_layer=(l == DEPTH - 1))
    return xf.reshape(B, S, D_MODEL)


def kernel(x_prompt, x_sample, norm_gain, w_in, lb_logits, hgrn_norm_gain, q_norm_gain, w_uq,
           kv_norm_gain, w_ukv, w_pa, w_pb, w_out, final_norm_gain):
    lbs = _lower_bounds(lb_logits)
    layers = [_prep_layer_weights(w_uq[l], w_ukv[l], w_pa[l], w_pb[l], w_out[l]) for l in range(DEPTH)]
    args = (_pack_w_in(w_in), layers, lbs, norm_gain, hgrn_norm_gain, q_norm_gain, kv_norm_gain,
            final_norm_gain)
    return (_trunk(x_prompt, *args), _trunk(x_sample, *args))
```
